```python
import jax, jax.numpy as jnp
from jax import lax
import numpy as np

D_MODEL = 1024
BATCH = 2
SEQ = 8192
DEPTH = 1

CHUNK = 64
Q_BLOCK = 128
EPS = 1e-6
D_ATTN = D_MODEL // 2
HEAD_DIM = 64
N_HEADS = D_ATTN // HEAD_DIM
N_IDX_HEADS = 4
IDX_DIM = 64
TOPK_MAX = 256
D_POOL = D_MODEL // 2
POOL_WINDOWS = (2, 4, 8, 16)
N_POOL_GROUPS = len(POOL_WINDOWS)
POOL_GROUP_DIM = D_POOL // N_POOL_GROUPS
N_BRANCH = 2
N_EXPERTS = 32
TOP_K_EXPERTS = 4
D_EXPERT = D_MODEL
SWIGLU_LIMIT = 7.0
SWIGLU_ALPHA = 1.702
_WIDTHS = (D_ATTN, D_ATTN, D_ATTN, N_IDX_HEADS * IDX_DIM, IDX_DIM, N_IDX_HEADS, D_POOL, N_BRANCH * D_MODEL)
D_IN_PROJ = sum(_WIDTHS)
SPLIT_OFFSETS = tuple(int(v) for v in np.cumsum(_WIDTHS)[:-1])

kernel_name = "hybrid_dsa_pool_moe_block"


def _rmsnorm(x, g):
    xf = x.astype(jnp.float32)
    xf = xf * lax.rsqrt(jnp.mean(xf * xf, axis=-1, keepdims=True) + EPS)
    return (xf * g.astype(jnp.float32)).astype(x.dtype)


def _gather_rows(kb, ib):
    return kb[ib]


def _dsa_attention(q, k, v, iq, ik, iw, q_norm_g, k_norm_g):
    B_, S_ = q.shape[0], q.shape[1]
    L = S_
    topk = min(TOPK_MAX, L // 4)
    n_blocks = S_ // Q_BLOCK
    q = _rmsnorm(q, q_norm_g)
    k = _rmsnorm(k, k_norm_g)
    scale = HEAD_DIM ** -0.5
    idx_scale = IDX_DIM ** -0.5
    iw = iw.astype(jnp.float32) * (N_IDX_HEADS ** -0.5)
    key_pos = jnp.arange(S_)

    def to_blocks(a):
        return jnp.swapaxes(a.reshape((B_, n_blocks, Q_BLOCK) + a.shape[2:]), 0, 1)

    def one_block(args):
        qb, iqb, iwb, blk = args
        t = blk * Q_BLOCK + jnp.arange(Q_BLOCK)
        limit = (t // CHUNK + 1) * CHUNK
        admissible = key_pos[None, :] < limit[:, None]
        dots = jnp.einsum('bthd,bsd->bths', iqb, ik).astype(jnp.float32) * idx_scale
        score = jnp.einsum('bths,bth->bts', jax.nn.relu(dots), iwb)
        score = jnp.where(admissible[None], score, -jnp.inf)
        _, sel = lax.top_k(score, topk)
        valid = sel < limit[None, :, None]
        ksel = jax.vmap(_gather_rows)(k, sel)
        vsel = jax.vmap(_gather_rows)(v, sel)
        logits = jnp.einsum('bthd,btkhd->bhtk', qb, ksel).astype(jnp.float32) * scale
        logits = jnp.where(valid[:, None], logits, -jnp.inf)
        p = jax.nn.softmax(logits, axis=-1)
        return jnp.einsum('bhtk,btkhd->bthd', p.astype(vsel.dtype), vsel)

    out = lax.map(one_block, (to_blocks(q), to_blocks(iq), to_blocks(iw), jnp.arange(n_blocks)))
    out = jnp.swapaxes(out, 0, 1).reshape(B_, S_, N_HEADS * HEAD_DIM)
    return out


def _pool_mixer(u, pool_w, pool_b, pool_scale):
    B_, S_, _ = u.shape
    uf = u.astype(jnp.float32)
    csum = jnp.concatenate([jnp.zeros((B_, 1, D_POOL), jnp.float32), jnp.cumsum(uf, axis=1)], axis=1)
    t = jnp.arange(S_)
    groups = []
    for g, w in enumerate(POOL_WINDOWS):
        sl = slice(g * POOL_GROUP_DIM, (g + 1) * POOL_GROUP_DIM)
        lo = jnp.maximum(t + 1 - w, 0)
        cnt = (t + 1 - lo).astype(jnp.float32)
        c = csum[:, :, sl]
        mean = (c[:, 1:] - jnp.take(c, lo, axis=1)) / cnt[None, :, None]
        groups.append(mean - uf[:, :, sl])
    y = jnp.stack(groups, axis=2).astype(u.dtype)
    y = jnp.einsum('bsgc,gcd->bsgd', y, pool_w).reshape(B_, S_, D_POOL)
    return (y + pool_b) * pool_scale


def _moe(h, router_w, router_b, w1, b1, w2, b2):
    B_, S_, D_ = h.shape
    tok = h.reshape(-1, D_)
    logits = (tok @ router_w).astype(jnp.float32) + router_b.astype(jnp.float32)
    topv, topi = lax.top_k(logits, TOP_K_EXPERTS)
    topw = jax.nn.softmax(topv, axis=-1)
    combine = jnp.sum(jax.nn.one_hot(topi, N_EXPERTS, dtype=jnp.float32) * topw[..., None], axis=1)
    out = jnp.zeros(tok.shape, jnp.float32)
    for e in range(N_EXPERTS):
        gu = tok @ w1[e] + b1[e]
        gate = jnp.minimum(gu[:, :D_EXPERT], SWIGLU_LIMIT)
        lin = jnp.clip(gu[:, D_EXPERT:], -SWIGLU_LIMIT, SWIGLU_LIMIT)
        act = (lin + 1) * gate * jax.nn.sigmoid(SWIGLU_ALPHA * gate)
        out = out + combine[:, e:e + 1] * (act @ w2[e] + b2[e])
    return out.reshape(B_, S_, D_).astype(h.dtype)


def setup_inputs(seed: int = 0) -> dict:
    key = jax.random.key(seed)
    ks = jax.random.split(key, 20)
    f32 = jnp.float32
    nrm = lambda k, shape, s: jax.random.normal(k, shape, f32) * s
    return {
        "x": nrm(ks[0], (BATCH, SEQ, D_MODEL), 1.0),
        "attn_norm_g": 1.0 + nrm(ks[1], (DEPTH, D_MODEL), 0.05),
        "w_in": nrm(ks[2], (DEPTH, D_MODEL, D_IN_PROJ), D_MODEL ** -0.5),
        "q_norm_g": 1.0 + nrm(ks[3], (DEPTH, HEAD_DIM), 0.05),
        "k_norm_g": 1.0 + nrm(ks[4], (DEPTH, HEAD_DIM), 0.05),
        "pool_w": nrm(ks[5], (DEPTH, N_POOL_GROUPS, POOL_GROUP_DIM, POOL_GROUP_DIM), POOL_GROUP_DIM ** -0.5),
        "pool_b": nrm(ks[6], (DEPTH, D_POOL), 0.02),
        "pool_scale": 1.0 + nrm(ks[7], (DEPTH, D_POOL), 0.05),
        "w_branch_attn": nrm(ks[8], (DEPTH, D_ATTN, D_MODEL), D_ATTN ** -0.5),
        "w_branch_pool": nrm(ks[9], (DEPTH, D_POOL, D_MODEL), D_POOL ** -0.5),
        "w_out": nrm(ks[10], (DEPTH, D_MODEL, D_MODEL), D_MODEL ** -0.5),
        "ffn_norm_g": 1.0 + nrm(ks[11], (DEPTH, D_MODEL), 0.05),
        "router_w": nrm(ks[12], (DEPTH, D_MODEL, N_EXPERTS), D_MODEL ** -0.5),
        "router_b": nrm(ks[13], (DEPTH, N_EXPERTS), 0.01),
        "expert_w1": nrm(ks[14], (DEPTH, N_EXPERTS, D_MODEL, 2 * D_EXPERT), D_MODEL ** -0.5),
        "expert_b1": nrm(ks[15], (DEPTH, N_EXPERTS, 2 * D_EXPERT), 0.02),
        "expert_w2": nrm(ks[16], (DEPTH, N_EXPERTS, D_EXPERT, D_MODEL), D_EXPERT ** -0.5),
        "expert_b2": nrm(ks[17], (DEPTH, N_EXPERTS, D_MODEL), 0.02),
    }


def reference(x, attn_norm_g, w_in, q_norm_g, k_norm_g, pool_w, pool_b, pool_scale,
              w_branch_attn, w_branch_pool, w_out, ffn_norm_g, router_w, router_b,
              expert_w1, expert_b1, expert_w2, expert_b2):
    B_, S_, _ = x.shape
    for l in range(DEPTH):
        h = _rmsnorm(x, attn_norm_g[l])
        proj = h @ w_in[l]
        q, k, v, iq, ik, iw, u, gates = jnp.split(proj, SPLIT_OFFSETS, axis=-1)
        q = q.reshape(B_, S_, N_HEADS, HEAD_DIM)
        k = k.reshape(B_, S_, N_HEADS, HEAD_DIM)
        v = v.reshape(B_, S_, N_HEADS, HEAD_DIM)
        iq = iq.reshape(B_, S_, N_IDX_HEADS, IDX_DIM)
        y_attn = _dsa_attention(q, k, v, iq, ik, iw, q_norm_g[l], k_norm_g[l])
        y_pool = _pool_mixer(u, pool_w[l], pool_b[l], pool_scale[l])
        g_attn, g_pool = jnp.split(jax.nn.sigmoid(gates.astype(jnp.float32)), N_BRANCH, axis=-1)
        merged = g_attn * (y_attn @ w_branch_attn[l]) + g_pool * (y_pool @ w_branch_pool[l])
        x = x + merged.astype(x.dtype) @ w_out[l]
        h2 = _rmsnorm(x, ffn_norm_g[l])
        x = x + _moe(h2, router_w[l], router_b[l], expert_w1[l], expert_b1[l], expert_w2[l], expert_b2[l])
    return x
```

```python
import functools

import jax
import jax.numpy as jnp
from jax import lax
from jax.experimental import pallas as pl
from jax.experimental.pallas import tpu as pltpu

F32 = jnp.float32
BF16 = jnp.bfloat16
I32 = jnp.int32

EPS = 1e-6
CHUNK = 64
HEAD_DIM = 64
N_HEADS = 8
N_IDX_HEADS = 4
IDX_DIM = 64
TOPK_MAX = 256
POOL_WINDOWS = (2, 4, 8, 16)
POOL_HALO = 16
N_EXPERTS = 32
TOP_K_EXPERTS = 4
SWIGLU_LIMIT = 7.0
SWIGLU_ALPHA = 1.702

LANES = 128
NEG_BIG = -1e30
INT_MIN = -(2 ** 31)
KEY_NEG_FLT_MAX = INT_MIN + 0x00800000
VMEM_LIMIT = 56 * 1024 * 1024


def _cparams(*sem):
    return pltpu.CompilerParams(dimension_semantics=sem, vmem_limit_bytes=VMEM_LIMIT)


def _dot(a, b):
    return jnp.dot(a, b, preferred_element_type=F32)


def _inproj_kernel(x_ref, g_ref, w_ref, gq_ref, gk_ref, bd_ref,
                   q_ref, k_ref, v_ref, iq_ref, ik_ref, iw_ref, u_ref, sg_ref, *, d_attn, d_pool):
    x = x_ref[...]
    h = x * lax.rsqrt(jnp.mean(x * x, axis=-1, keepdims=True) + EPS) * g_ref[...]
    hb = h.astype(BF16)
    bd = bd_ref[...]

    def proj(lo, hi):
        return _dot(hb, w_ref[:, lo:hi])

    def head_norm(t, g, scale):
        t2 = t * t
        hi = t2.astype(BF16)
        lo = (t2 - hi.astype(F32)).astype(BF16)
        ss = _dot(hi, bd) + _dot(lo, bd)
        return t * lax.rsqrt(ss * (1.0 / HEAD_DIM) + EPS) * (g * scale)

    o = 0
    q_ref[...] = head_norm(proj(o, o + d_attn), gq_ref[...], HEAD_DIM ** -0.5).astype(BF16)
    o += d_attn
    k_ref[...] = head_norm(proj(o, o + d_attn), gk_ref[...], 1.0).astype(BF16)
    o += d_attn
    v_ref[...] = proj(o, o + d_attn).astype(BF16)
    o += d_attn
    idx = proj(o, o + 4 * LANES)
    iq_ref[...] = (idx[:, :2 * LANES] * (IDX_DIM ** -0.5)).astype(BF16)
    ik_ref[...] = idx[:, 2 * LANES:3 * LANES].astype(BF16)
    iw_ref[...] = idx[:, 3 * LANES:] * (N_IDX_HEADS ** -0.5)
    o += 4 * LANES
    u_ref[...] = proj(o, o + d_pool)
    o += d_pool
    sg_ref[...] = jax.nn.sigmoid(proj(o, w_ref.shape[1]))


def _inproj(x2, g, w_cat, gq, gk, bd, *, tm, d_attn, d_pool, d_model):
    T = x2.shape[0]
    n = w_cat.shape[1]
    const = lambda i: (0, 0)
    row = lambda i: (i, 0)
    outs = [
        jax.ShapeDtypeStruct((T, d_attn), BF16),
        jax.ShapeDtypeStruct((T, d_attn), BF16),
        jax.ShapeDtypeStruct((T, d_attn), BF16),
        jax.ShapeDtypeStruct((T, 2 * LANES), BF16),
        jax.ShapeDtypeStruct((T, LANES), BF16),
        jax.ShapeDtypeStruct((T, LANES), F32),
        jax.ShapeDtypeStruct((T, d_pool), F32),
        jax.ShapeDtypeStruct((T, 2 * d_model), F32),
    ]
    return pl.pallas_call(
        functools.partial(_inproj_kernel, d_attn=d_attn, d_pool=d_pool),
        grid=(T // tm,),
        in_specs=[
            pl.BlockSpec((tm, d_model), row),
            pl.BlockSpec((1, d_model), const),
            pl.BlockSpec((d_model, n), const),
            pl.BlockSpec((1, d_attn), const),
            pl.BlockSpec((1, d_attn), const),
            pl.BlockSpec((d_attn, d_attn), const),
        ],
        out_specs=[pl.BlockSpec((tm, s.shape[1]), row) for s in outs],
        out_shape=outs,
        compiler_params=_cparams("arbitrary"),
        name="inproj",
    )(x2, g, w_cat, gq, gk, bd)


def _key_to_float(k):
    bits = jnp.where(k >= 0, k, k ^ 0x7FFFFFFF)
    return lax.bitcast_convert_type(bits, F32)


def _dsa_kernel(q_ref, iq_ref, iw_ref, kt_ref, v_ref, ikt_ref, tri_ref, o_ref, sc_ref, *, tq, tk, topk):
    qi = pl.program_id(1)
    row0 = qi * tq
    nvis = (row0 + tq + tk - 1) // tk
    rows = row0 + lax.broadcasted_iota(I32, (tq, 1), 0)
    limit = (rows // CHUNK + 1) * CHUNK
    lane = lax.broadcasted_iota(I32, (tq, LANES), 1)
    low = lane < HEAD_DIM
    n_sub = tk // LANES

    def blk(kb):
        return pl.ds(pl.multiple_of(kb * tk, tk), tk)

    iq = iq_ref[...]
    iw = iw_ref[...]
    zero_b = jnp.zeros((), BF16)
    iq_h = []
    for j in range(N_IDX_HEADS // 2):
        pair = iq[:, j * LANES:(j + 1) * LANES]
        iq_h.append(jnp.where(low, pair, zero_b))
        iq_h.append(jnp.where(low, zero_b, pair))
    iw_b = [jnp.broadcast_to(iw[:, h:h + 1], (tq, tk)) for h in range(N_IDX_HEADS)]
    col = lax.broadcasted_iota(I32, (tq, tk), 1)

    def score_body(kb, carry):
        ikt = ikt_ref[:, blk(kb)]
        s = jnp.zeros((tq, tk), F32)
        for h in range(N_IDX_HEADS):
            s = s + iw_b[h] * jnp.maximum(_dot(iq_h[h], ikt), 0.0)
        s = jnp.where(col + kb * tk < limit, s, -jnp.inf)
        sc_ref[:, blk(kb)] = s
        return carry

    lax.fori_loop(0, nvis, score_body, 0)

    def count(cand, strict):
        def body(kb, acc):
            s = sc_ref[:, blk(kb)]
            hit = (s > cand) if strict else (s >= cand)
            m = jnp.where(hit, 1.0, 0.0)
            for j in range(n_sub):
                acc = acc + m[:, j * LANES:(j + 1) * LANES]
            return acc
        acc = lax.fori_loop(0, nvis, body, jnp.zeros((tq, LANES), F32))
        return jnp.sum(acc, axis=1, keepdims=True)

    kf = float(topk)
    c0 = jnp.where(count(jnp.zeros((tq, 1), F32), False) >= kf, 0, INT_MIN).astype(I32)

    def search_body(i, c):
        cand = c | jnp.left_shift(jnp.int32(1), 30 - i)
        return jnp.where(count(_key_to_float(cand), False) >= kf, cand, c)

    ckey = lax.fori_loop(0, 31, search_body, c0)
    thr = _key_to_float(jnp.maximum(ckey, KEY_NEG_FLT_MAX))
    need = kf - count(thr, True)

    q = q_ref[...]
    qm = []
    for j in range(N_HEADS // 2):
        pair = q[:, j * LANES:(j + 1) * LANES]
        qm.append(jnp.where(low, pair, zero_b))
        qm.append(jnp.where(low, zero_b, pair))
    tri = tri_ref[...]

    def attn_body(kb, st):
        ms, ls, accs, carry = st
        s = sc_ref[:, blk(kb)]
        eq = s == thr
        eqf = jnp.where(eq, 1.0, 0.0)
        before = carry + _dot(eqf.astype(BF16), tri)
        sel = (s > thr) | (eq & (before < need))
        carry = carry + jnp.sum(eqf, axis=1, keepdims=True)
        new_ms, new_ls, new_accs = [], [], []
        for j in range(N_HEADS // 2):
            kt = kt_ref[j * LANES:(j + 1) * LANES, blk(kb)]
            vv = v_ref[blk(kb), j * LANES:(j + 1) * LANES]
            acc = accs[j]
            for e in range(2):
                h = 2 * j + e
                lg = jnp.where(sel, _dot(qm[h], kt), NEG_BIG)
                m_new = jnp.maximum(ms[h], jnp.max(lg, axis=1, keepdims=True))
                alpha = jnp.exp(ms[h] - m_new)
                p = jnp.exp(lg - m_new)
                new_ls.append(alpha * ls[h] + jnp.sum(p, axis=1, keepdims=True))
                new_ms.append(m_new)
                pv = _dot(p.astype(BF16), vv)
                half = low if e == 0 else jnp.logical_not(low)
                acc = jnp.where(half, alpha * acc + pv, acc)
            new_accs.append(acc)
        return new_ms, new_ls, new_accs, carry

    st0 = ([jnp.full((tq, 1), NEG_BIG, F32)] * N_HEADS,
           [jnp.zeros((tq, 1), F32)] * N_HEADS,
           [jnp.zeros((tq, LANES), F32)] * (N_HEADS // 2),
           jnp.zeros((tq, 1), F32))
    _, ls, accs, _ = lax.fori_loop(0, nvis, attn_body, st0)
    for j in range(N_HEADS // 2):
        inv = jnp.where(low, 1.0 / ls[2 * j], 1.0 / ls[2 * j + 1])
        o_ref[:, j * LANES:(j + 1) * LANES] = (accs[j] * inv).astype(BF16)


def _dsa(q, iq, iw, kt, v, ikt, tri, *, tq, tk, topk):
    B, S, d_attn = q.shape
    return pl.pallas_call(
        functools.partial(_dsa_kernel, tq=tq, tk=tk, topk=topk),
        grid=(B, S // tq),
        in_specs=[
            pl.BlockSpec((None, tq, d_attn), lambda b, i: (b, i, 0)),
            pl.BlockSpec((None, tq, 2 * LANES), lambda b, i: (b, i, 0)),
            pl.BlockSpec((None, tq, LANES), lambda b, i: (b, i, 0)),
            pl.BlockSpec((None, d_attn, S), lambda b, i: (b, 0, 0)),
            pl.BlockSpec((None, S, d_attn), lambda b, i: (b, 0, 0)),
            pl.BlockSpec((None, LANES, S), lambda b, i: (b, 0, 0)),
            pl.BlockSpec((tk, tk), lambda b, i: (0, 0)),
        ],
        out_specs=pl.BlockSpec((None, tq, d_attn), lambda b, i: (b, i, 0)),
        out_shape=jax.ShapeDtypeStruct((B, S, d_attn), BF16),
        scratch_shapes=[pltpu.VMEM((tq, S), F32)],
        compiler_params=_cparams("arbitrary", "arbitrary"),
        name="dsa",
    )(q, iq, iw, kt, v, ikt, tri)


def _mix_kernel(x_ref, u_ref, uh_ref, sg_ref, ya_ref, pw_ref, pb_ref, ps_ref, wa_ref, wp_ref, wo_ref,
                g2_ref, rw_ref, rb_ref, x1_ref, h2_ref, lg_ref, *, ts, tiles_per_seq, d_model):
    it = pl.program_id(0) % tiles_per_seq
    u = u_ref[...]
    halo = jnp.where(it == 0, 0.0, uh_ref[...])
    ext = jnp.concatenate([halo, u], axis=0)
    t_in_seq = it * ts + lax.broadcasted_iota(I32, (ts, 1), 0)
    ys = []
    for g, w in enumerate(POOL_WINDOWS):
        s = ext[:, g * LANES:(g + 1) * LANES]
        d = 1
        while d < w:
            s = s + pltpu.roll(s, d, 0)
            d *= 2
        cnt = jnp.minimum(t_in_seq + 1, w).astype(F32)
        y = s[POOL_HALO:, :] / cnt - u[:, g * LANES:(g + 1) * LANES]
        ys.append(_dot(y.astype(BF16), pw_ref[g]))
    y_pool = (jnp.concatenate(ys, axis=1) + pb_ref[...]) * ps_ref[...]
    sg = sg_ref[...]
    merged = (sg[:, :d_model] * _dot(ya_ref[...], wa_ref[...])
              + sg[:, d_model:] * _dot(y_pool.astype(BF16), wp_ref[...]))
    x1 = x_ref[...] + _dot(merged.astype(BF16), wo_ref[...])
    x1_ref[...] = x1
    h2 = x1 * lax.rsqrt(jnp.mean(x1 * x1, axis=-1, keepdims=True) + EPS) * g2_ref[...]
    h2_ref[...] = h2
    lg_ref[...] = _dot(h2.astype(BF16), rw_ref[...]) + rb_ref[...]


def _mix(x2, u, sg, ya, pw, pb, ps, wa, wp, wo, g2, rw, rb, *, ts, seq):
    T, d_model = x2.shape
    d_pool = u.shape[1]
    d_attn = ya.shape[1]
    tiles_per_seq = seq // ts
    halo_blocks = ts // POOL_HALO
    const2 = lambda i: (0, 0)
    row = lambda i: (i, 0)
    outs = [
        jax.ShapeDtypeStruct((T, d_model), F32),
        jax.ShapeDtypeStruct((T, d_model), F32),
        jax.ShapeDtypeStruct((T, LANES), F32),
    ]
    return pl.pallas_call(
        functools.partial(_mix_kernel, ts=ts, tiles_per_seq=tiles_per_seq, d_model=d_model),
        grid=(T // ts,),
        in_specs=[
            pl.BlockSpec((ts, d_model), row),
            pl.BlockSpec((ts, d_pool), row),
            pl.BlockSpec((POOL_HALO, d_pool), lambda i: (jnp.maximum(i * halo_blocks - 1, 0), 0)),
            pl.BlockSpec((ts, 2 * d_model), row),
            pl.BlockSpec((ts, d_attn), row),
            pl.BlockSpec(pw.shape, lambda i: (0, 0, 0)),
            pl.BlockSpec((1, d_pool), const2),
            pl.BlockSpec((1, d_pool), const2),
            pl.BlockSpec(wa.shape, const2),
            pl.BlockSpec(wp.shape, const2),
            pl.BlockSpec(wo.shape, const2),
            pl.BlockSpec((1, d_model), const2),
            pl.BlockSpec(rw.shape, const2),
            pl.BlockSpec((1, LANES), const2),
        ],
        out_specs=[pl.BlockSpec((ts, s.shape[1]), row) for s in outs],
        out_shape=outs,
        compiler_params=_cparams("arbitrary"),
        name="mix",
    )(x2, u, u, sg, ya, pw, pb, ps, wa, wp, wo, g2, rw, rb)


def _router_kernel(lg_ref, lower_ref, idx_ref, w_ref, rank_ref, cnt_ref, carry_ref, *, ts):
    @pl.when(pl.program_id(0) == 0)
    def _():
        carry_ref[...] = jnp.zeros_like(carry_ref)

    l = lg_ref[...]
    lane = lax.broadcasted_iota(I32, (ts, LANES), 1)
    vals, idxs, hots = [], [], []
    for _ in range(TOP_K_EXPERTS):
        m = jnp.max(l, axis=1, keepdims=True)
        idx = jnp.min(jnp.where(l == m, lane, LANES), axis=1, keepdims=True)
        hot = lane == idx
        vals.append(m)
        idxs.append(idx)
        hots.append(hot)
        l = jnp.where(hot, -jnp.inf, l)
    es = [jnp.exp(v - vals[0]) for v in vals]
    den = es[0] + es[1] + es[2] + es[3]
    picked = jnp.zeros((ts, LANES), F32)
    for hot in hots:
        picked = picked + jnp.where(hot, 1.0, 0.0)
    before = carry_ref[...] + _dot(lower_ref[...], picked.astype(BF16))
    carry_ref[...] = carry_ref[...] + jnp.sum(picked, axis=0, keepdims=True)
    cnt_ref[...] = carry_ref[...]
    idx_o = jnp.zeros((ts, LANES), I32)
    w_o = jnp.zeros((ts, LANES), F32)
    rank_o = jnp.zeros((ts, LANES), F32)
    for k in range(TOP_K_EXPERTS):
        slot = lane == k
        rank_k = jnp.sum(jnp.where(hots[k], before, 0.0), axis=1, keepdims=True)
        idx_o = jnp.where(slot, idxs[k], idx_o)
        w_o = jnp.where(slot, es[k] / den, w_o)
        rank_o = jnp.where(slot, rank_k, rank_o)
    idx_ref[...] = idx_o
    w_ref[...] = w_o
    rank_ref[...] = rank_o.astype(I32)


def _router(logits, lower, *, ts):
    T = logits.shape[0]
    row = lambda i: (i, 0)
    const = lambda i: (0, 0)
    outs = [
        jax.ShapeDtypeStruct((T, LANES), I32),
        jax.ShapeDtypeStruct((T, LANES), F32),
        jax.ShapeDtypeStruct((T, LANES), I32),
        jax.ShapeDtypeStruct((1, LANES), F32),
    ]
    return pl.pallas_call(
        functools.partial(_router_kernel, ts=ts),
        grid=(T // ts,),
        in_specs=[pl.BlockSpec((ts, LANES), row), pl.BlockSpec((ts, ts), const)],
        out_specs=[pl.BlockSpec((ts, LANES), row)] * 3 + [pl.BlockSpec((1, LANES), const)],
        out_shape=outs,
        scratch_shapes=[pltpu.VMEM((1, LANES), F32)],
        compiler_params=_cparams("arbitrary"),
        name="router",
    )(logits, lower)


def _dispatch_kernel(dest_ref, h_ref, xs_in_ref, xs_ref, sem, *, ts):
    del xs_in_ref
    n = ts * TOP_K_EXPERTS

    def row_copy(j):
        r = j // TOP_K_EXPERTS
        return pltpu.make_async_copy(h_ref.at[pl.ds(r, 1)], xs_ref.at[pl.ds(dest_ref[0, 0, j], 1)], sem)

    def start(j, c):
        row_copy(j).start()
        return c

    def wait(j, c):
        row_copy(j).wait()
        return c

    lax.fori_loop(0, n, start, 0)
    lax.fori_loop(0, n, wait, 0)


def _dispatch(dest3, h2, xs_init, *, ts):
    T, d = h2.shape
    return pl.pallas_call(
        functools.partial(_dispatch_kernel, ts=ts),
        grid=(T // ts,),
        in_specs=[
            pl.BlockSpec((1, 1, ts * TOP_K_EXPERTS), lambda i: (i, 0, 0), memory_space=pltpu.SMEM),
            pl.BlockSpec((ts, d), lambda i: (i, 0)),
            pl.BlockSpec(memory_space=pl.ANY),
        ],
        out_specs=pl.BlockSpec(memory_space=pl.ANY),
        out_shape=jax.ShapeDtypeStruct(xs_init.shape, xs_init.dtype),
        scratch_shapes=[pltpu.SemaphoreType.DMA(())],
        input_output_aliases={2: 0},
        compiler_params=_cparams("arbitrary"),
        name="dispatch",
    )(dest3, h2, xs_init)


def _expert_kernel(te_ref, tv_ref, x_ref, w1_ref, b1_ref, w2_ref, b2_ref, y_ref, *, d_expert):
    i = pl.program_id(0)

    @pl.when(tv_ref[i] == 1)
    def _():
        gu = _dot(x_ref[...].astype(BF16), w1_ref[...]) + b1_ref[...]
        gate = jnp.minimum(gu[:, :d_expert], SWIGLU_LIMIT)
        lin = jnp.clip(gu[:, d_expert:], -SWIGLU_LIMIT, SWIGLU_LIMIT)
        act = (lin + 1.0) * gate * jax.nn.sigmoid(SWIGLU_ALPHA * gate)
        y_ref[...] = _dot(act.astype(BF16), w2_ref[...]) + b2_ref[...]

    @pl.when(tv_ref[i] == 0)
    def _():
        y_ref[...] = jnp.zeros_like(y_ref)


def _experts(tile_expert, tile_valid, xs, w1, b1, w2, b2, *, tm):
    P, d = xs.shape
    d_expert = w2.shape[1]
    grid_spec = pltpu.PrefetchScalarGridSpec(
        num_scalar_prefetch=2,
        grid=(P // tm,),
        in_specs=[
            pl.BlockSpec((tm, d), lambda i, te, tv: (i, 0)),
            pl.BlockSpec((None, d, 2 * d_expert), lambda i, te, tv: (te[i], 0, 0)),
            pl.BlockSpec((None, 1, 2 * d_expert), lambda i, te, tv: (te[i], 0, 0)),
            pl.BlockSpec((None, d_expert, d), lambda i, te, tv: (te[i], 0, 0)),
            pl.BlockSpec((None, 1, d), lambda i, te, tv: (te[i], 0, 0)),
        ],
        out_specs=pl.BlockSpec((tm, d), lambda i, te, tv: (i, 0)),
    )
    return pl.pallas_call(
        functools.partial(_expert_kernel, d_expert=d_expert),
        grid_spec=grid_spec,
        out_shape=jax.ShapeDtypeStruct((P, d), F32),
        compiler_params=_cparams("arbitrary"),
        name="experts",
    )(tile_expert, tile_valid, xs, w1, b1, w2, b2)


def _combine_kernel(dest_ref, x1_ref, w_ref, y_ref, o_ref, buf_ref, sem, *, ts):
    n = ts * TOP_K_EXPERTS

    def row_copy(j):
        r = j // TOP_K_EXPERTS
        k = j % TOP_K_EXPERTS
        return pltpu.make_async_copy(y_ref.at[pl.ds(dest_ref[0, 0, j], 1)], buf_ref.at[k, pl.ds(r, 1)], sem)

    def start(j, c):
        row_copy(j).start()
        return c

    def wait(j, c):
        row_copy(j).wait()
        return c

    lax.fori_loop(0, n, start, 0)
    lax.fori_loop(0, n, wait, 0)
    w = w_ref[...]
    out = x1_ref[...]
    for k in range(TOP_K_EXPERTS):
        out = out + w[:, k:k + 1] * buf_ref[k]
    o_ref[...] = out


def _combine(dest3, x1, wts, y, *, ts):
    T, d = x1.shape
    return pl.pallas_call(
        functools.partial(_combine_kernel, ts=ts),
        grid=(T // ts,),
        in_specs=[
            pl.BlockSpec((1, 1, ts * TOP_K_EXPERTS), lambda i: (i, 0, 0), memory_space=pltpu.SMEM),
            pl.BlockSpec((ts, d), lambda i: (i, 0)),
            pl.BlockSpec((ts, LANES), lambda i: (i, 0)),
            pl.BlockSpec(memory_space=pl.ANY),
        ],
        out_specs=pl.BlockSpec((ts, d), lambda i: (i, 0)),
        out_shape=jax.ShapeDtypeStruct((T, d), F32),
        scratch_shapes=[pltpu.VMEM((TOP_K_EXPERTS, ts, d), F32), pltpu.SemaphoreType.DMA(())],
        compiler_params=_cparams("arbitrary"),
        name="combine",
    )(dest3, x1, wts, y)


def _pick(n, prefs):
    for p in prefs:
        if n % p == 0:
            return p
    raise ValueError(f"no tile in {prefs} divides {n}")


def _layer(x, attn_norm_g, w_in, q_norm_g, k_norm_g, pool_w, pool_b, pool_scale, w_branch_attn,
           w_branch_pool, w_out, ffn_norm_g, router_w, router_b, w1, b1, w2, b2):
    B, S, d_model = x.shape
    T = B * S
    d_attn = N_HEADS * HEAD_DIM
    d_pool = len(POOL_WINDOWS) * LANES
    d_idx = N_IDX_HEADS * IDX_DIM
    assert d_attn == d_model // 2 and d_pool == d_model // 2 and d_idx == 2 * LANES
    assert w_in.shape[1] == 3 * d_attn + d_idx + IDX_DIM + N_IDX_HEADS + d_pool + 2 * d_model

    o = 3 * d_attn
    w_iq = w_in[:, o:o + d_idx]
    w_ik = w_in[:, o + d_idx:o + d_idx + IDX_DIM]
    w_iw = w_in[:, o + d_idx + IDX_DIM:o + d_idx + IDX_DIM + N_IDX_HEADS]
    o2 = o + d_idx + IDX_DIM + N_IDX_HEADS
    w_cat = jnp.concatenate([
        w_in[:, :o], w_iq,
        jnp.pad(w_ik, ((0, 0), (0, LANES - IDX_DIM))),
        jnp.pad(w_iw, ((0, 0), (0, LANES - N_IDX_HEADS))),
        w_in[:, o2:]], axis=1).astype(BF16)
    head_of = jnp.arange(d_attn) // HEAD_DIM
    bd = (head_of[:, None] == head_of[None, :]).astype(BF16)

    x2 = x.reshape(T, d_model)
    q, k, v, iq, ik, iw, u, sg = _inproj(
        x2, attn_norm_g.reshape(1, d_model), w_cat,
        jnp.tile(q_norm_g, N_HEADS).reshape(1, d_attn), jnp.tile(k_norm_g, N_HEADS).reshape(1, d_attn), bd,
        tm=_pick(T, (256, 128)), d_attn=d_attn, d_pool=d_pool, d_model=d_model)

    topk = min(TOPK_MAX, S // 4)
    tq = _pick(S, (128,))
    tk = _pick(S, (512, 256, 128))
    kt = jnp.swapaxes(k.reshape(B, S, d_attn), 1, 2)
    ikt = jnp.swapaxes(ik.reshape(B, S, LANES)[:, :, :IDX_DIM], 1, 2)
    ikt2 = jnp.concatenate([ikt, ikt], axis=1)
    tri = (jnp.arange(tk)[:, None] < jnp.arange(tk)[None, :]).astype(BF16)
    y_attn = _dsa(q.reshape(B, S, d_attn), iq.reshape(B, S, 2 * LANES), iw.reshape(B, S, LANES),
                  kt, v.reshape(B, S, d_attn), ikt2, tri, tq=tq, tk=tk, topk=topk)

    rw = jnp.pad(router_w, ((0, 0), (0, LANES - N_EXPERTS))).astype(BF16)
    rb = jnp.concatenate([router_b.astype(F32), jnp.full((LANES - N_EXPERTS,), NEG_BIG, F32)]).reshape(1, LANES)
    x1, h2, logits = _mix(
        x2, u, sg, y_attn.reshape(T, d_attn), pool_w.astype(BF16), pool_b.reshape(1, d_pool),
        pool_scale.reshape(1, d_pool), w_branch_attn.astype(BF16), w_branch_pool.astype(BF16),
        w_out.astype(BF16), ffn_norm_g.reshape(1, d_model), rw, rb, ts=_pick(S, (256, 128)), seq=S)

    tr = _pick(T, (512, 256, 128))
    lower = (jnp.arange(tr)[:, None] > jnp.arange(tr)[None, :]).astype(BF16)
    idx, wts, rank, cnt = _router(logits, lower, ts=tr)

    tm = 256
    counts = cnt[0, :N_EXPERTS].astype(I32)
    padded = (counts + tm - 1) // tm * tm
    ends = jnp.cumsum(padded)
    starts = ends - padded
    n_tiles = (T * TOP_K_EXPERTS) // tm + N_EXPERTS
    P = n_tiles * tm
    tile_row = jnp.arange(n_tiles, dtype=I32) * tm
    tile_valid = (tile_row < ends[-1]).astype(I32)
    tile_expert = jnp.minimum(jnp.searchsorted(ends, tile_row, side="right"), N_EXPERTS - 1).astype(I32)
    dest = starts[idx[:, :TOP_K_EXPERTS]] + rank[:, :TOP_K_EXPERTS]

    td = _pick(T, (256, 128))
    xs = _dispatch(dest.reshape(T // td, 1, td * TOP_K_EXPERTS), h2, jnp.zeros((P, d_model), F32), ts=td)
    y = _experts(tile_expert, tile_valid, xs, w1.astype(BF16), b1.reshape(N_EXPERTS, 1, -1),
                 w2.astype(BF16), b2.reshape(N_EXPERTS, 1, -1), tm=tm)
    tc = _pick(T, (128,))
    out = _combine(dest.reshape(T // tc, 1, tc * TOP_K_EXPERTS), x1, wts, y, ts=tc)
    return out.reshape(B, S, d_model)


def kernel(x, attn_norm_g, w_in, q_norm_g, k_norm_g, pool_w, pool_b, pool_scale, w_branch_attn, w_branch_pool,
           w_out, ffn_norm_g, router_w, router_b, expert_w1, expert_b1, expert_w2, expert_b2):
    for l in range(attn_norm_g.shape[0]):
        x = _layer(x, attn_norm_g[l], w_in[l], q_norm_g[l], k_norm_g[l], pool_w[l], pool_b[l], pool_scale[l],
                   w_branch_attn[l], w_branch_pool[l], w_out[l], ffn_norm_g[l], router_w[l], router_b[l],
                   expert_w1[l], expert_b1[l], expert_w2[l], expert_b2[l])
    return x
```

```python
import functools

import jax
import jax.numpy as jnp
from jax import lax
from jax.experimental import pallas as pl
from jax.experimental.pallas import tpu as pltpu

F32 = jnp.float32
BF16 = jnp.bfloat16
I32 = jnp.int32

EPS = 1e-6
CHUNK = 64
HEAD_DIM = 64
N_HEADS = 8
N_IDX_HEADS = 4
IDX_DIM = 64
TOPK_MAX = 256
POOL_WINDOWS = (2, 4, 8, 16)
POOL_HALO = 16
N_EXPERTS = 32
TOP_K_EXPERTS = 4
SWIGLU_LIMIT = 7.0
SWIGLU_ALPHA = 1.702

LANES = 128
COUNT_ROWS = 64
NEG_BIG = -1e30
INT_MIN = -(2 ** 31)
KEY_NEG_FLT_MAX = INT_MIN + 0x00800000
VMEM_LIMIT = 56 * 1024 * 1024


def _cparams(*sem):
    return pltpu.CompilerParams(dimension_semantics=sem, vmem_limit_bytes=VMEM_LIMIT)


def _dot(a, b):
    return jnp.dot(a, b, preferred_element_type=F32)


def _inproj_kernel(x_ref, g_ref, w_ref, gq_ref, gk_ref, bd_ref,
                   q_ref, k_ref, v_ref, iq_ref, ik_ref, iw_ref, u_ref, sg_ref, *, d_attn, d_pool):
    x = x_ref[...]
    h = x * lax.rsqrt(jnp.mean(x * x, axis=-1, keepdims=True) + EPS) * g_ref[...]
    hb = h.astype(BF16)
    bd = bd_ref[...]

    def proj(lo, hi):
        return _dot(hb, w_ref[:, lo:hi])

    def head_norm(t, g, scale):
        t2 = t * t
        hi = t2.astype(BF16)
        lo = (t2 - hi.astype(F32)).astype(BF16)
        ss = _dot(hi, bd) + _dot(lo, bd)
        return t * lax.rsqrt(ss * (1.0 / HEAD_DIM) + EPS) * (g * scale)

    o = 0
    q_ref[...] = head_norm(proj(o, o + d_attn), gq_ref[...], HEAD_DIM ** -0.5).astype(BF16)
    o += d_attn
    k_ref[...] = head_norm(proj(o, o + d_attn), gk_ref[...], 1.0).astype(BF16)
    o += d_attn
    v_ref[...] = proj(o, o + d_attn).astype(BF16)
    o += d_attn
    idx = proj(o, o + 4 * LANES)
    iq_ref[...] = (idx[:, :2 * LANES] * (IDX_DIM ** -0.5)).astype(BF16)
    ik_ref[...] = idx[:, 2 * LANES:3 * LANES].astype(BF16)
    iw_ref[...] = idx[:, 3 * LANES:] * (N_IDX_HEADS ** -0.5)
    o += 4 * LANES
    u_ref[...] = proj(o, o + d_pool)
    o += d_pool
    sg_ref[...] = jax.nn.sigmoid(proj(o, w_ref.shape[1]))


def _inproj(x2, g, w_cat, gq, gk, bd, *, tm, d_attn, d_pool, d_model):
    T = x2.shape[0]
    n = w_cat.shape[1]
    const = lambda i: (0, 0)
    row = lambda i: (i, 0)
    outs = [
        jax.ShapeDtypeStruct((T, d_attn), BF16),
        jax.ShapeDtypeStruct((T, d_attn), BF16),
        jax.ShapeDtypeStruct((T, d_attn), BF16),
        jax.ShapeDtypeStruct((T, 2 * LANES), BF16),
        jax.ShapeDtypeStruct((T, LANES), BF16),
        jax.ShapeDtypeStruct((T, LANES), F32),
        jax.ShapeDtypeStruct((T, d_pool), F32),
        jax.ShapeDtypeStruct((T, 2 * d_model), F32),
    ]
    return pl.pallas_call(
        functools.partial(_inproj_kernel, d_attn=d_attn, d_pool=d_pool),
        grid=(T // tm,),
        in_specs=[
            pl.BlockSpec((tm, d_model), row),
            pl.BlockSpec((1, d_model), const),
            pl.BlockSpec((d_model, n), const),
            pl.BlockSpec((1, d_attn), const),
            pl.BlockSpec((1, d_attn), const),
            pl.BlockSpec((d_attn, d_attn), const),
        ],
        out_specs=[pl.BlockSpec((tm, s.shape[1]), row) for s in outs],
        out_shape=outs,
        compiler_params=_cparams("arbitrary"),
        name="inproj",
    )(x2, g, w_cat, gq, gk, bd)


def _key_to_float(k):
    bits = jnp.where(k >= 0, k, k ^ 0x7FFFFFFF)
    return lax.bitcast_convert_type(bits, F32)


def _dsa_kernel(q_ref, iq_ref, iw_ref, kt_ref, v_ref, ikt_ref, tri_ref, o_ref, sc_ref, m_ref, acc_ref,
                *, tq, tk, topk):
    qi = pl.program_id(1)
    row0 = qi * tq
    nvis = (row0 + tq + tk - 1) // tk
    rows = row0 + lax.broadcasted_iota(I32, (tq, 1), 0)
    limit = (rows // CHUNK + 1) * CHUNK
    lane = lax.broadcasted_iota(I32, (tq, LANES), 1)
    low = lane < HEAD_DIM
    n_sub = tk // LANES
    zero_b = jnp.zeros((), BF16)
    one_b = jnp.ones((), BF16)

    def blk(kb):
        return pl.ds(pl.multiple_of(kb * tk, tk), tk)

    def head_masked(pair):
        return [jnp.where(low, pair, zero_b), jnp.where(low, zero_b, pair)]

    iq = iq_ref[...]
    iw = iw_ref[...]
    iq_h = []
    for j in range(N_IDX_HEADS // 2):
        iq_h += head_masked(iq[:, j * LANES:(j + 1) * LANES])
    iw_b = [jnp.broadcast_to(iw[:, h:h + 1], (tq, tk)) for h in range(N_IDX_HEADS)]
    col = lax.broadcasted_iota(I32, (tq, tk), 1)

    def score_body(kb, carry):
        ikt = ikt_ref[:, blk(kb)]
        s = jnp.zeros((tq, tk), F32)
        for h in range(N_IDX_HEADS):
            s = s + iw_b[h] * jnp.maximum(_dot(iq_h[h], ikt), 0.0)
        s = jnp.where(col + kb * tk < limit, s, -jnp.inf)
        sc_ref[:, blk(kb)] = s
        return carry

    lax.fori_loop(0, nvis, score_body, 0)

    n_rc = tq // COUNT_ROWS

    def count(cand, strict):
        cands = [jnp.broadcast_to(cand[r * COUNT_ROWS:(r + 1) * COUNT_ROWS], (COUNT_ROWS, LANES))
                 for r in range(n_rc)]

        def body(kb, accs):
            out = []
            for r in range(n_rc):
                acc = accs[r]
                for j in range(n_sub):
                    s = sc_ref[r * COUNT_ROWS:(r + 1) * COUNT_ROWS,
                               pl.ds(pl.multiple_of(kb * tk + j * LANES, LANES), LANES)]
                    hit = (s > cands[r]) if strict else (s >= cands[r])
                    acc = acc + jnp.where(hit, 1.0, 0.0)
                out.append(acc)
            return tuple(out)

        accs = lax.fori_loop(0, nvis, body, tuple(jnp.zeros((COUNT_ROWS, LANES), F32) for _ in range(n_rc)))
        return jnp.concatenate([jnp.sum(a, axis=1, keepdims=True) for a in accs], axis=0)

    kf = float(topk)
    cnt0 = count(jnp.zeros((tq, 1), F32), False)
    ok0 = cnt0 >= kf
    c0 = jnp.where(ok0, 0, INT_MIN).astype(I32)
    n0 = jnp.where(ok0, cnt0, 0.0)

    def search_body(i, st):
        c, n_ge = st
        cand = c | jnp.left_shift(jnp.int32(1), 30 - i)
        cnt = count(_key_to_float(cand), False)
        ok = cnt >= kf
        return jnp.where(ok, cand, c), jnp.where(ok, cnt, n_ge)

    ckey, n_ge = lax.fori_loop(0, 31, search_body, (c0, n0))
    thr = _key_to_float(jnp.maximum(ckey, KEY_NEG_FLT_MAX))
    need = kf - count(thr, True)
    ambiguous = jnp.max(jnp.where(n_ge > kf, 1.0, 0.0)) > 0.0

    @pl.when(jnp.logical_not(ambiguous))
    def _():
        def body(kb, carry):
            sc_ref[:, blk(kb)] = jnp.where(sc_ref[:, blk(kb)] >= thr, 0.0, NEG_BIG)
            return carry
        lax.fori_loop(0, nvis, body, 0)

    @pl.when(ambiguous)
    def _():
        tri = tri_ref[...]

        def body(kb, carry):
            s = sc_ref[:, blk(kb)]
            eq = s == thr
            eqf = jnp.where(eq, 1.0, 0.0)
            before = carry + _dot(eqf.astype(BF16), tri)
            sel = (s > thr) | (eq & (before < need))
            sc_ref[:, blk(kb)] = jnp.where(sel, 0.0, NEG_BIG)
            return carry + jnp.sum(eqf, axis=1, keepdims=True)
        lax.fori_loop(0, nvis, body, jnp.zeros((tq, 1), F32))

    q = q_ref[...]
    low_k = lax.broadcasted_iota(I32, (tk, LANES), 1) < HEAD_DIM
    for j in range(N_HEADS // 2):
        qm = head_masked(q[:, j * LANES:(j + 1) * LANES])
        m_ref[...] = jnp.full((2, tq, LANES), NEG_BIG, F32)
        acc_ref[...] = jnp.zeros((2, tq, LANES), F32)

        def logits(kb, j=j, qm=qm):
            kt = kt_ref[j * LANES:(j + 1) * LANES, blk(kb)]
            return tuple(_dot(qm[e], kt) for e in range(2))

        def body(kb, lgs, j=j):
            nxt = logits(jnp.minimum(kb + 1, nvis - 1))
            vv = v_ref[blk(kb), j * LANES:(j + 1) * LANES]
            bias = sc_ref[:, blk(kb)]
            for e in range(2):
                vh = jnp.where(low_k if e == 0 else jnp.logical_not(low_k), vv, one_b)
                lg = lgs[e] + bias
                m_old = m_ref[e]
                m_new = jnp.maximum(m_old, jnp.max(lg, axis=1, keepdims=True))
                p = jnp.concatenate(
                    [jnp.exp(lg[:, c * LANES:(c + 1) * LANES] - m_new) for c in range(n_sub)], axis=1)
                acc_ref[e] = jnp.exp(m_old - m_new) * acc_ref[e] + _dot(p.astype(BF16), vh)
                m_ref[e] = m_new
            return nxt

        lax.fori_loop(0, nvis, body, logits(0))
        outs = []
        for e in range(2):
            acc = acc_ref[e]
            outs.append(acc / pltpu.roll(acc, HEAD_DIM, 1))
        o_ref[:, j * LANES:(j + 1) * LANES] = jnp.where(low, outs[0], outs[1]).astype(BF16)


def _dsa(q, iq, iw, kt, v, ikt, tri, *, tq, tk, topk):
    B, S, d_attn = q.shape
    once = dict(pipeline_mode=pl.Buffered(1))
    return pl.pallas_call(
        functools.partial(_dsa_kernel, tq=tq, tk=tk, topk=topk),
        grid=(B, S // tq),
        in_specs=[
            pl.BlockSpec((None, tq, d_attn), lambda b, i: (b, i, 0)),
            pl.BlockSpec((None, tq, 2 * LANES), lambda b, i: (b, i, 0)),
            pl.BlockSpec((None, tq, LANES), lambda b, i: (b, i, 0)),
            pl.BlockSpec((None, d_attn, S), lambda b, i: (b, 0, 0), **once),
            pl.BlockSpec((None, S, d_attn), lambda b, i: (b, 0, 0), **once),
            pl.BlockSpec((None, LANES, S), lambda b, i: (b, 0, 0), **once),
            pl.BlockSpec((tk, tk), lambda b, i: (0, 0), **once),
        ],
        out_specs=pl.BlockSpec((None, tq, d_attn), lambda b, i: (b, i, 0)),
        out_shape=jax.ShapeDtypeStruct((B, S, d_attn), BF16),
        scratch_shapes=[pltpu.VMEM((tq, S), F32), pltpu.VMEM((2, tq, LANES), F32), pltpu.VMEM((2, tq, LANES), F32)],
        compiler_params=_cparams("arbitrary", "arbitrary"),
        name="dsa",
    )(q, iq, iw, kt, v, ikt, tri)


def _mix_kernel(x_ref, u_ref, uh_ref, sg_ref, ya_ref, pw_ref, pb_ref, ps_ref, wa_ref, wp_ref, wo_ref,
                g2_ref, rw_ref, rb_ref, x1_ref, h2_ref, lg_ref, *, ts, tiles_per_seq, d_model):
    it = pl.program_id(0) % tiles_per_seq
    u = u_ref[...]
    halo = jnp.where(it == 0, 0.0, uh_ref[...])
    ext = jnp.concatenate([halo, u], axis=0)
    t_in_seq = it * ts + lax.broadcasted_iota(I32, (ts, 1), 0)
    ys = []
    for g, w in enumerate(POOL_WINDOWS):
        s = ext[:, g * LANES:(g + 1) * LANES]
        d = 1
        while d < w:
            s = s + pltpu.roll(s, d, 0)
            d *= 2
        cnt = jnp.minimum(t_in_seq + 1, w).astype(F32)
        y = s[POOL_HALO:, :] / cnt - u[:, g * LANES:(g + 1) * LANES]
        ys.append(_dot(y.astype(BF16), pw_ref[g]))
    y_pool = (jnp.concatenate(ys, axis=1) + pb_ref[...]) * ps_ref[...]
    sg = sg_ref[...]
    merged = (sg[:, :d_model] * _dot(ya_ref[...], wa_ref[...])
              + sg[:, d_model:] * _dot(y_pool.astype(BF16), wp_ref[...]))
    x1 = x_ref[...] + _dot(merged.astype(BF16), wo_ref[...])
    x1_ref[...] = x1
    h2 = x1 * lax.rsqrt(jnp.mean(x1 * x1, axis=-1, keepdims=True) + EPS) * g2_ref[...]
    h2_ref[...] = h2
    lg_ref[...] = _dot(h2.astype(BF16), rw_ref[...]) + rb_ref[...]


def _mix(x2, u, sg, ya, pw, pb, ps, wa, wp, wo, g2, rw, rb, *, ts, seq):
    T, d_model = x2.shape
    d_pool = u.shape[1]
    d_attn = ya.shape[1]
    tiles_per_seq = seq // ts
    halo_blocks = ts // POOL_HALO
    const2 = lambda i: (0, 0)
    row = lambda i: (i, 0)
    outs = [
        jax.ShapeDtypeStruct((T, d_model), F32),
        jax.ShapeDtypeStruct((T, d_model), F32),
        jax.ShapeDtypeStruct((T, LANES), F32),
    ]
    return pl.pallas_call(
        functools.partial(_mix_kernel, ts=ts, tiles_per_seq=tiles_per_seq, d_model=d_model),
        grid=(T // ts,),
        in_specs=[
            pl.BlockSpec((ts, d_model), row),
            pl.BlockSpec((ts, d_pool), row),
            pl.BlockSpec((POOL_HALO, d_pool), lambda i: (jnp.maximum(i * halo_blocks - 1, 0), 0)),
            pl.BlockSpec((ts, 2 * d_model), row),
            pl.BlockSpec((ts, d_attn), row),
            pl.BlockSpec(pw.shape, lambda i: (0, 0, 0)),
            pl.BlockSpec((1, d_pool), const2),
            pl.BlockSpec((1, d_pool), const2),
            pl.BlockSpec(wa.shape, const2),
            pl.BlockSpec(wp.shape, const2),
            pl.BlockSpec(wo.shape, const2),
            pl.BlockSpec((1, d_model), const2),
            pl.BlockSpec(rw.shape, const2),
            pl.BlockSpec((1, LANES), const2),
        ],
        out_specs=[pl.BlockSpec((ts, s.shape[1]), row) for s in outs],
        out_shape=outs,
        compiler_params=_cparams("arbitrary"),
        name="mix",
    )(x2, u, u, sg, ya, pw, pb, ps, wa, wp, wo, g2, rw, rb)


def _router_kernel(lg_ref, lower_ref, idx_ref, w_ref, rank_ref, cnt_ref, carry_ref, *, ts):
    @pl.when(pl.program_id(0) == 0)
    def _():
        carry_ref[...] = jnp.zeros_like(carry_ref)

    l = lg_ref[...]
    lane = lax.broadcasted_iota(I32, (ts, LANES), 1)
    vals, idxs, hots = [], [], []
    for _ in range(TOP_K_EXPERTS):
        m = jnp.max(l, axis=1, keepdims=True)
        idx = jnp.min(jnp.where(l == m, lane, LANES), axis=1, keepdims=True)
        hot = lane == idx
        vals.append(m)
        idxs.append(idx)
        hots.append(hot)
        l = jnp.where(hot, -jnp.inf, l)
    es = [jnp.exp(v - vals[0]) for v in vals]
    den = es[0] + es[1] + es[2] + es[3]
    picked = jnp.zeros((ts, LANES), F32)
    for hot in hots:
        picked = picked + jnp.where(hot, 1.0, 0.0)
    before = carry_ref[...] + _dot(lower_ref[...], picked.astype(BF16))
    carry_ref[...] = carry_ref[...] + jnp.sum(picked, axis=0, keepdims=True)
    cnt_ref[...] = carry_ref[...]
    idx_o = jnp.zeros((ts, LANES), I32)
    w_o = jnp.zeros((ts, LANES), F32)
    rank_o = jnp.zeros((ts, LANES), F32)
    for k in range(TOP_K_EXPERTS):
        slot = lane == k
        rank_k = jnp.sum(jnp.where(hots[k], before, 0.0), axis=1, keepdims=True)
        idx_o = jnp.where(slot, idxs[k], idx_o)
        w_o = jnp.where(slot, es[k] / den, w_o)
        rank_o = jnp.where(slot, rank_k, rank_o)
    idx_ref[...] = idx_o
    w_ref[...] = w_o
    rank_ref[...] = rank_o.astype(I32)


def _router(logits, lower, *, ts):
    T = logits.shape[0]
    row = lambda i: (i, 0)
    const = lambda i: (0, 0)
    outs = [
        jax.ShapeDtypeStruct((T, LANES), I32),
        jax.ShapeDtypeStruct((T, LANES), F32),
        jax.ShapeDtypeStruct((T, LANES), I32),
        jax.ShapeDtypeStruct((1, LANES), F32),
    ]
    return pl.pallas_call(
        functools.partial(_router_kernel, ts=ts),
        grid=(T // ts,),
        in_specs=[pl.BlockSpec((ts, LANES), row), pl.BlockSpec((ts, ts), const)],
        out_specs=[pl.BlockSpec((ts, LANES), row)] * 3 + [pl.BlockSpec((1, LANES), const)],
        out_shape=outs,
        scratch_shapes=[pltpu.VMEM((1, LANES), F32)],
        compiler_params=_cparams("arbitrary"),
        name="router",
    )(logits, lower)


def _dispatch_kernel(dest_ref, h_ref, xs_in_ref, xs_ref, sem, *, ts):
    del xs_in_ref
    n = ts * TOP_K_EXPERTS

    def row_copy(j):
        r = j // TOP_K_EXPERTS
        return pltpu.make_async_copy(h_ref.at[pl.ds(r, 1)], xs_ref.at[pl.ds(dest_ref[0, 0, j], 1)], sem)

    def start(j, c):
        row_copy(j).start()
        return c

    def wait(j, c):
        row_copy(j).wait()
        return c

    lax.fori_loop(0, n, start, 0)
    lax.fori_loop(0, n, wait, 0)


def _dispatch(dest3, h2, xs_init, *, ts):
    T, d = h2.shape
    return pl.pallas_call(
        functools.partial(_dispatch_kernel, ts=ts),
        grid=(T // ts,),
        in_specs=[
            pl.BlockSpec((1, 1, ts * TOP_K_EXPERTS), lambda i: (i, 0, 0), memory_space=pltpu.SMEM),
            pl.BlockSpec((ts, d), lambda i: (i, 0)),
            pl.BlockSpec(memory_space=pl.ANY),
        ],
        out_specs=pl.BlockSpec(memory_space=pl.ANY),
        out_shape=jax.ShapeDtypeStruct(xs_init.shape, xs_init.dtype),
        scratch_shapes=[pltpu.SemaphoreType.DMA(())],
        input_output_aliases={2: 0},
        compiler_params=_cparams("arbitrary"),
        name="dispatch",
    )(dest3, h2, xs_init)


def _expert_kernel(te_ref, tv_ref, x_ref, w1_ref, b1_ref, w2_ref, b2_ref, y_ref, *, d_expert):
    i = pl.program_id(0)

    @pl.when(tv_ref[i] == 1)
    def _():
        gu = _dot(x_ref[...].astype(BF16), w1_ref[...]) + b1_ref[...]
        gate = jnp.minimum(gu[:, :d_expert], SWIGLU_LIMIT)
        lin = jnp.clip(gu[:, d_expert:], -SWIGLU_LIMIT, SWIGLU_LIMIT)
        act = (lin + 1.0) * gate * jax.nn.sigmoid(SWIGLU_ALPHA * gate)
        y_ref[...] = _dot(act.astype(BF16), w2_ref[...]) + b2_ref[...]

    @pl.when(tv_ref[i] == 0)
    def _():
        y_ref[...] = jnp.zeros_like(y_ref)


def _experts(tile_expert, tile_valid, xs, w1, b1, w2, b2, *, tm):
    P, d = xs.shape
    d_expert = w2.shape[1]
    grid_spec = pltpu.PrefetchScalarGridSpec(
        num_scalar_prefetch=2,
        grid=(P // tm,),
        in_specs=[
            pl.BlockSpec((tm, d), lambda i, te, tv: (i, 0)),
            pl.BlockSpec((None, d, 2 * d_expert), lambda i, te, tv: (te[i], 0, 0)),
            pl.BlockSpec((None, 1, 2 * d_expert), lambda i, te, tv: (te[i], 0, 0)),
            pl.BlockSpec((None, d_expert, d), lambda i, te, tv: (te[i], 0, 0)),
            pl.BlockSpec((None, 1, d), lambda i, te, tv: (te[i], 0, 0)),
        ],
        out_specs=pl.BlockSpec((tm, d), lambda i, te, tv: (i, 0)),
    )
    return pl.pallas_call(
        functools.partial(_expert_kernel, d_expert=d_expert),
        grid_spec=grid_spec,
        out_shape=jax.ShapeDtypeStruct((P, d), F32),
        compiler_params=_cparams("arbitrary"),
        name="experts",
    )(tile_expert, tile_valid, xs, w1, b1, w2, b2)


def _combine_kernel(dest_ref, x1_ref, w_ref, y_ref, o_ref, buf_ref, sem, *, ts):
    n = ts * TOP_K_EXPERTS

    def row_copy(j):
        r = j // TOP_K_EXPERTS
        k = j % TOP_K_EXPERTS
        return pltpu.make_async_copy(y_ref.at[pl.ds(dest_ref[0, 0, j], 1)], buf_ref.at[k, pl.ds(r, 1)], sem)

    def start(j, c):
        row_copy(j).start()
        return c

    def wait(j, c):
        row_copy(j).wait()
        return c

    lax.fori_loop(0, n, start, 0)
    lax.fori_loop(0, n, wait, 0)
    w = w_ref[...]
    out = x1_ref[...]
    for k in range(TOP_K_EXPERTS):
        out = out + w[:, k:k + 1] * buf_ref[k]
    o_ref[...] = out


def _combine(dest3, x1, wts, y, *, ts):
    T, d = x1.shape
    return pl.pallas_call(
        functools.partial(_combine_kernel, ts=ts),
        grid=(T // ts,),
        in_specs=[
            pl.BlockSpec((1, 1, ts * TOP_K_EXPERTS), lambda i: (i, 0, 0), memory_space=pltpu.SMEM),
            pl.BlockSpec((ts, d), lambda i: (i, 0)),
            pl.BlockSpec((ts, LANES), lambda i: (i, 0)),
            pl.BlockSpec(memory_space=pl.ANY),
        ],
        out_specs=pl.BlockSpec((ts, d), lambda i: (i, 0)),
        out_shape=jax.ShapeDtypeStruct((T, d), F32),
        scratch_shapes=[pltpu.VMEM((TOP_K_EXPERTS, ts, d), F32), pltpu.SemaphoreType.DMA(())],
        compiler_params=_cparams("arbitrary"),
        name="combine",
    )(dest3, x1, wts, y)


def _pick(n, prefs):
    for p in prefs:
        if n % p == 0:
            return p
    raise ValueError(f"no tile in {prefs} divides {n}")


def _layer(x, attn_norm_g, w_in, q_norm_g, k_norm_g, pool_w, pool_b, pool_scale, w_branch_attn,
           w_branch_pool, w_out, ffn_norm_g, router_w, router_b, w1, b1, w2, b2):
    B, S, d_model = x.shape
    T = B * S
    d_attn = N_HEADS * HEAD_DIM
    d_pool = len(POOL_WINDOWS) * LANES
    d_idx = N_IDX_HEADS * IDX_DIM
    assert d_attn == d_model // 2 and d_pool == d_model // 2 and d_idx == 2 * LANES
    assert w_in.shape[1] == 3 * d_attn + d_idx + IDX_DIM + N_IDX_HEADS + d_pool + 2 * d_model

    o = 3 * d_attn
    w_iq = w_in[:, o:o + d_idx]
    w_ik = w_in[:, o + d_idx:o + d_idx + IDX_DIM]
    w_iw = w_in[:, o + d_idx + IDX_DIM:o + d_idx + IDX_DIM + N_IDX_HEADS]
    o2 = o + d_idx + IDX_DIM + N_IDX_HEADS
    w_cat = jnp.concatenate([
        w_in[:, :o], w_iq,
        jnp.pad(w_ik, ((0, 0), (0, LANES - IDX_DIM))),
        jnp.pad(w_iw, ((0, 0), (0, LANES - N_IDX_HEADS))),
        w_in[:, o2:]], axis=1).astype(BF16)
    head_of = jnp.arange(d_attn) // HEAD_DIM
    bd = (head_of[:, None] == head_of[None, :]).astype(BF16)

    x2 = x.reshape(T, d_model)
    q, k, v, iq, ik, iw, u, sg = _inproj(
        x2, attn_norm_g.reshape(1, d_model), w_cat,
        jnp.tile(q_norm_g, N_HEADS).reshape(1, d_attn), jnp.tile(k_norm_g, N_HEADS).reshape(1, d_attn), bd,
        tm=_pick(T, (256, 128)), d_attn=d_attn, d_pool=d_pool, d_model=d_model)

    topk = min(TOPK_MAX, S // 4)
    tq = _pick(S, (256, 128))
    tk = _pick(S, (512, 256, 128))
    kt = jnp.swapaxes(k.reshape(B, S, d_attn), 1, 2)
    ikt = jnp.swapaxes(ik.reshape(B, S, LANES)[:, :, :IDX_DIM], 1, 2)
    ikt2 = jnp.concatenate([ikt, ikt], axis=1)
    tri = (jnp.arange(tk)[:, None] < jnp.arange(tk)[None, :]).astype(BF16)
    y_attn = _dsa(q.reshape(B, S, d_attn), iq.reshape(B, S, 2 * LANES), iw.reshape(B, S, LANES),
                  kt, v.reshape(B, S, d_attn), ikt2, tri, tq=tq, tk=tk, topk=topk)

    rw = jnp.pad(router_w, ((0, 0), (0, LANES - N_EXPERTS))).astype(BF16)
    rb = jnp.concatenate([router_b.astype(F32), jnp.full((LANES - N_EXPERTS,), NEG_BIG, F32)]).reshape(1, LANES)
    x1, h2, logits = _mix(
        x2, u, sg, y_attn.reshape(T, d_attn), pool_w.astype(BF16), pool_b.reshape(1, d_pool),
        pool_scale.reshape(1, d_pool), w_branch_attn.astype(BF16), w_branch_pool.astype(BF16),
        w_out.astype(BF16), ffn_norm_g.reshape(1, d_model), rw, rb, ts=_pick(S, (256, 128)), seq=S)

    tr = _pick(T, (512, 256, 128))
    lower = (jnp.arange(tr)[:, None] > jnp.arange(tr)[None, :]).astype(BF16)
    idx, wts, rank, cnt = _router(logits, lower, ts=tr)

    tm = 256
    counts = cnt[0, :N_EXPERTS].astype(I32)
    padded = (counts + tm - 1) // tm * tm
    ends = jnp.cumsum(padded)
    starts = ends - padded
    n_tiles = (T * TOP_K_EXPERTS) // tm + N_EXPERTS
    P = n_tiles * tm
    tile_row = jnp.arange(n_tiles, dtype=I32) * tm
    tile_valid = (tile_row < ends[-1]).astype(I32)
    tile_expert = jnp.minimum(jnp.sum((tile_row[:, None] >= ends[None, :]).astype(I32), axis=1), N_EXPERTS - 1)
    dest = starts[idx[:, :TOP_K_EXPERTS]] + rank[:, :TOP_K_EXPERTS]

    td = _pick(T, (256, 128))
    xs = _dispatch(dest.reshape(T // td, 1, td * TOP_K_EXPERTS), h2, jnp.zeros((P, d_model), F32), ts=td)
    y = _experts(tile_expert, tile_valid, xs, w1.astype(BF16), b1.reshape(N_EXPERTS, 1, -1),
                 w2.astype(BF16), b2.reshape(N_EXPERTS, 1, -1), tm=tm)
    tc = _pick(T, (128,))
    out = _combine(dest.reshape(T // tc, 1, tc * TOP_K_EXPERTS), x1, wts, y, ts=tc)
    return out.reshape(B, S, d_model)


def kernel(x, attn_norm_g, w_in, q_norm_g, k_norm_g, pool_w, pool_b, pool_scale, w_branch_attn, w_branch_pool,
           w_out, ffn_norm_g, router_w, router_b, expert_w1, expert_b1, expert_w2, expert_b2):
    for l in range(attn_norm_g.shape[0]):
        x = _layer(x, attn_norm_g[l], w_in[l], q_norm_g[l], k_norm_g[l], pool_w[l], pool_b[l], pool_scale[l],
                   w_branch_attn[l], w_branch_pool[l], w_out[l], ffn_norm_g[l], router_w[l], router_b[l],
                   expert_w1[l], expert_b1[l], expert_w2[l], expert_b2[l])
    return x
```

```python
import functools

import jax
import jax.numpy as jnp
from jax import lax
from jax.experimental import pallas as pl
from jax.experimental.pallas import tpu as pltpu

F32 = jnp.float32
BF16 = jnp.bfloat16
I32 = jnp.int32

EPS = 1e-6
CHUNK = 64
HEAD_DIM = 64
N_HEADS = 8
N_IDX_HEADS = 4
IDX_DIM = 64
TOPK_MAX = 256
POOL_WINDOWS = (2, 4, 8, 16)
POOL_HALO = 16
N_EXPERTS = 32
TOP_K_EXPERTS = 4
SWIGLU_LIMIT = 7.0
SWIGLU_ALPHA = 1.702

LANES = 128
COUNT_ROWS = 64
ROW_ALIGN = 8
PERM_ROWS = 256
EXPERT_ROWS = 256
NEG_BIG = -1e30
INT_MIN = -(2 ** 31)
KEY_NEG_FLT_MAX = INT_MIN + 0x00800000
VMEM_LIMIT = 56 * 1024 * 1024


def _cparams(*sem):
    return pltpu.CompilerParams(dimension_semantics=sem, vmem_limit_bytes=VMEM_LIMIT)


def _dot(a, b):
    return jnp.dot(a, b, preferred_element_type=F32)


def _inproj_kernel(x_ref, g_ref, w_ref, gq_ref, gk_ref, bd_ref,
                   q_ref, k_ref, v_ref, iq_ref, ik_ref, iw_ref, u_ref, sg_ref, *, d_attn, d_pool):
    x = x_ref[...]
    h = x * lax.rsqrt(jnp.mean(x * x, axis=-1, keepdims=True) + EPS) * g_ref[...]
    hb = h.astype(BF16)
    bd = bd_ref[...]

    def proj(lo, hi):
        return _dot(hb, w_ref[:, lo:hi])

    def head_norm(t, g, scale):
        t2 = t * t
        hi = t2.astype(BF16)
        lo = (t2 - hi.astype(F32)).astype(BF16)
        ss = _dot(hi, bd) + _dot(lo, bd)
        return t * lax.rsqrt(ss * (1.0 / HEAD_DIM) + EPS) * (g * scale)

    o = 0
    q_ref[...] = head_norm(proj(o, o + d_attn), gq_ref[...], HEAD_DIM ** -0.5).astype(BF16)
    o += d_attn
    k_ref[...] = head_norm(proj(o, o + d_attn), gk_ref[...], 1.0).astype(BF16)
    o += d_attn
    v_ref[...] = proj(o, o + d_attn).astype(BF16)
    o += d_attn
    idx = proj(o, o + 4 * LANES)
    iq_ref[...] = (idx[:, :2 * LANES] * (IDX_DIM ** -0.5)).astype(BF16)
    ik_ref[...] = idx[:, 2 * LANES:3 * LANES].astype(BF16)
    iw_ref[...] = idx[:, 3 * LANES:] * (N_IDX_HEADS ** -0.5)
    o += 4 * LANES
    u_ref[...] = proj(o, o + d_pool)
    o += d_pool
    sg_ref[...] = jax.nn.sigmoid(proj(o, w_ref.shape[1]))


def _inproj(x2, g, w_cat, gq, gk, bd, *, tm, d_attn, d_pool, d_model):
    T = x2.shape[0]
    n = w_cat.shape[1]
    const = lambda i: (0, 0)
    row = lambda i: (i, 0)
    outs = [
        jax.ShapeDtypeStruct((T, d_attn), BF16),
        jax.ShapeDtypeStruct((T, d_attn), BF16),
        jax.ShapeDtypeStruct((T, d_attn), BF16),
        jax.ShapeDtypeStruct((T, 2 * LANES), BF16),
        jax.ShapeDtypeStruct((T, LANES), BF16),
        jax.ShapeDtypeStruct((T, LANES), F32),
        jax.ShapeDtypeStruct((T, d_pool), F32),
        jax.ShapeDtypeStruct((T, 2 * d_model), F32),
    ]
    return pl.pallas_call(
        functools.partial(_inproj_kernel, d_attn=d_attn, d_pool=d_pool),
        grid=(T // tm,),
        in_specs=[
            pl.BlockSpec((tm, d_model), row),
            pl.BlockSpec((1, d_model), const),
            pl.BlockSpec((d_model, n), const),
            pl.BlockSpec((1, d_attn), const),
            pl.BlockSpec((1, d_attn), const),
            pl.BlockSpec((d_attn, d_attn), const),
        ],
        out_specs=[pl.BlockSpec((tm, s.shape[1]), row) for s in outs],
        out_shape=outs,
        compiler_params=_cparams("arbitrary"),
        name="inproj",
    )(x2, g, w_cat, gq, gk, bd)


def _key_to_float(k):
    bits = jnp.where(k >= 0, k, k ^ 0x7FFFFFFF)
    return lax.bitcast_convert_type(bits, F32)


def _dsa_kernel(q_ref, iq_ref, iw_ref, kt_ref, v_ref, ikt_ref, tri_ref, o_ref, sc_ref, m_ref, acc_ref,
                *, tq, tk, topk):
    qi = pl.program_id(1)
    row0 = qi * tq
    nvis = (row0 + tq + tk - 1) // tk
    rows = row0 + lax.broadcasted_iota(I32, (tq, 1), 0)
    limit = (rows // CHUNK + 1) * CHUNK
    lane = lax.broadcasted_iota(I32, (tq, LANES), 1)
    low = lane < HEAD_DIM
    n_sub = tk // LANES
    zero_b = jnp.zeros((), BF16)
    one_b = jnp.ones((), BF16)

    def blk(kb):
        return pl.ds(pl.multiple_of(kb * tk, tk), tk)

    def head_masked(pair):
        return [jnp.where(low, pair, zero_b), jnp.where(low, zero_b, pair)]

    iq = iq_ref[...]
    iw = iw_ref[...]
    iq_h = []
    for j in range(N_IDX_HEADS // 2):
        iq_h += head_masked(iq[:, j * LANES:(j + 1) * LANES])
    iw_b = [jnp.broadcast_to(iw[:, h:h + 1], (tq, tk)) for h in range(N_IDX_HEADS)]
    col = lax.broadcasted_iota(I32, (tq, tk), 1)

    def score_body(kb, carry):
        ikt = ikt_ref[:, blk(kb)]
        s = jnp.zeros((tq, tk), F32)
        for h in range(N_IDX_HEADS):
            s = s + iw_b[h] * jnp.maximum(_dot(iq_h[h], ikt), 0.0)
        s = jnp.where(col + kb * tk < limit, s, -jnp.inf)
        sc_ref[:, blk(kb)] = s
        return carry

    lax.fori_loop(0, nvis, score_body, 0)

    n_rc = tq // COUNT_ROWS

    def count(cand, strict):
        cands = [jnp.broadcast_to(cand[r * COUNT_ROWS:(r + 1) * COUNT_ROWS], (COUNT_ROWS, LANES))
                 for r in range(n_rc)]

        def body(kb, accs):
            out = []
            for r in range(n_rc):
                acc = accs[r]
                for j in range(n_sub):
                    s = sc_ref[r * COUNT_ROWS:(r + 1) * COUNT_ROWS,
                               pl.ds(pl.multiple_of(kb * tk + j * LANES, LANES), LANES)]
                    hit = (s > cands[r]) if strict else (s >= cands[r])
                    acc = acc + jnp.where(hit, 1.0, 0.0)
                out.append(acc)
            return tuple(out)

        accs = lax.fori_loop(0, nvis, body, tuple(jnp.zeros((COUNT_ROWS, LANES), F32) for _ in range(n_rc)))
        return jnp.concatenate([jnp.sum(a, axis=1, keepdims=True) for a in accs], axis=0)

    kf = float(topk)
    cnt0 = count(jnp.zeros((tq, 1), F32), False)
    ok0 = cnt0 >= kf
    c0 = jnp.where(ok0, 0, INT_MIN).astype(I32)
    n0 = jnp.where(ok0, cnt0, 0.0)

    def search_body(i, st):
        c, n_ge = st
        cand = c | jnp.left_shift(jnp.int32(1), 30 - i)
        cnt = count(_key_to_float(cand), False)
        ok = cnt >= kf
        return jnp.where(ok, cand, c), jnp.where(ok, cnt, n_ge)

    ckey, n_ge = lax.fori_loop(0, 31, search_body, (c0, n0))
    thr = _key_to_float(jnp.maximum(ckey, KEY_NEG_FLT_MAX))
    need = kf - count(thr, True)
    ambiguous = jnp.max(jnp.where(n_ge > kf, 1.0, 0.0)) > 0.0

    @pl.when(jnp.logical_not(ambiguous))
    def _():
        def body(kb, carry):
            sc_ref[:, blk(kb)] = jnp.where(sc_ref[:, blk(kb)] >= thr, 0.0, NEG_BIG)
            return carry
        lax.fori_loop(0, nvis, body, 0)

    @pl.when(ambiguous)
    def _():
        tri = tri_ref[...]

        def body(kb, carry):
            s = sc_ref[:, blk(kb)]
            eq = s == thr
            eqf = jnp.where(eq, 1.0, 0.0)
            before = carry + _dot(eqf.astype(BF16), tri)
            sel = (s > thr) | (eq & (before < need))
            sc_ref[:, blk(kb)] = jnp.where(sel, 0.0, NEG_BIG)
            return carry + jnp.sum(eqf, axis=1, keepdims=True)
        lax.fori_loop(0, nvis, body, jnp.zeros((tq, 1), F32))

    q = q_ref[...]
    low_k = lax.broadcasted_iota(I32, (tk, LANES), 1) < HEAD_DIM
    for j in range(N_HEADS // 2):
        qm = head_masked(q[:, j * LANES:(j + 1) * LANES])
        m_ref[...] = jnp.full((2, tq, LANES), NEG_BIG, F32)
        acc_ref[...] = jnp.zeros((2, tq, LANES), F32)

        def logits(kb, j=j, qm=qm):
            kt = kt_ref[j * LANES:(j + 1) * LANES, blk(kb)]
            return tuple(_dot(qm[e], kt) for e in range(2))

        def body(kb, lgs, j=j):
            nxt = logits(jnp.minimum(kb + 1, nvis - 1))
            vv = v_ref[blk(kb), j * LANES:(j + 1) * LANES]
            bias = sc_ref[:, blk(kb)]
            for e in range(2):
                vh = jnp.where(low_k if e == 0 else jnp.logical_not(low_k), vv, one_b)
                lg = lgs[e] + bias
                m_old = m_ref[e]
                m_new = jnp.maximum(m_old, jnp.max(lg, axis=1, keepdims=True))
                p = jnp.concatenate(
                    [jnp.exp(lg[:, c * LANES:(c + 1) * LANES] - m_new) for c in range(n_sub)], axis=1)
                acc_ref[e] = jnp.exp(m_old - m_new) * acc_ref[e] + _dot(p.astype(BF16), vh)
                m_ref[e] = m_new
            return nxt

        lax.fori_loop(0, nvis, body, logits(0))
        outs = []
        for e in range(2):
            acc = acc_ref[e]
            outs.append(acc / pltpu.roll(acc, HEAD_DIM, 1))
        o_ref[:, j * LANES:(j + 1) * LANES] = jnp.where(low, outs[0], outs[1]).astype(BF16)


def _dsa(q, iq, iw, kt, v, ikt, tri, *, tq, tk, topk):
    B, S, d_attn = q.shape
    once = dict(pipeline_mode=pl.Buffered(1))
    return pl.pallas_call(
        functools.partial(_dsa_kernel, tq=tq, tk=tk, topk=topk),
        grid=(B, S // tq),
        in_specs=[
            pl.BlockSpec((None, tq, d_attn), lambda b, i: (b, i, 0)),
            pl.BlockSpec((None, tq, 2 * LANES), lambda b, i: (b, i, 0)),
            pl.BlockSpec((None, tq, LANES), lambda b, i: (b, i, 0)),
            pl.BlockSpec((None, d_attn, S), lambda b, i: (b, 0, 0), **once),
            pl.BlockSpec((None, S, d_attn), lambda b, i: (b, 0, 0), **once),
            pl.BlockSpec((None, LANES, S), lambda b, i: (b, 0, 0), **once),
            pl.BlockSpec((tk, tk), lambda b, i: (0, 0), **once),
        ],
        out_specs=pl.BlockSpec((None, tq, d_attn), lambda b, i: (b, i, 0)),
        out_shape=jax.ShapeDtypeStruct((B, S, d_attn), BF16),
        scratch_shapes=[pltpu.VMEM((tq, S), F32), pltpu.VMEM((2, tq, LANES), F32), pltpu.VMEM((2, tq, LANES), F32)],
        compiler_params=_cparams("arbitrary", "arbitrary"),
        name="dsa",
    )(q, iq, iw, kt, v, ikt, tri)


def _mix_kernel(x_ref, u_ref, uh_ref, sg_ref, ya_ref, pw_ref, pb_ref, ps_ref, wa_ref, wp_ref, wo_ref,
                g2_ref, rw_ref, rb_ref, x1_ref, h2_ref, lg_ref, *, ts, tiles_per_seq, d_model):
    it = pl.program_id(0) % tiles_per_seq
    u = u_ref[...]
    halo = jnp.where(it == 0, 0.0, uh_ref[...])
    ext = jnp.concatenate([halo, u], axis=0)
    t_in_seq = it * ts + lax.broadcasted_iota(I32, (ts, 1), 0)
    ys = []
    for g, w in enumerate(POOL_WINDOWS):
        s = ext[:, g * LANES:(g + 1) * LANES]
        d = 1
        while d < w:
            s = s + pltpu.roll(s, d, 0)
            d *= 2
        cnt = jnp.minimum(t_in_seq + 1, w).astype(F32)
        y = s[POOL_HALO:, :] / cnt - u[:, g * LANES:(g + 1) * LANES]
        ys.append(_dot(y.astype(BF16), pw_ref[g]))
    y_pool = (jnp.concatenate(ys, axis=1) + pb_ref[...]) * ps_ref[...]
    sg = sg_ref[...]
    merged = (sg[:, :d_model] * _dot(ya_ref[...], wa_ref[...])
              + sg[:, d_model:] * _dot(y_pool.astype(BF16), wp_ref[...]))
    x1 = x_ref[...] + _dot(merged.astype(BF16), wo_ref[...])
    x1_ref[...] = x1
    h2 = x1 * lax.rsqrt(jnp.mean(x1 * x1, axis=-1, keepdims=True) + EPS) * g2_ref[...]
    h2b = h2.astype(BF16)
    h2_ref[...] = h2b
    lg_ref[...] = _dot(h2b, rw_ref[...]) + rb_ref[...]


def _mix(x2, u, sg, ya, pw, pb, ps, wa, wp, wo, g2, rw, rb, *, ts, seq):
    T, d_model = x2.shape
    d_pool = u.shape[1]
    d_attn = ya.shape[1]
    tiles_per_seq = seq // ts
    halo_blocks = ts // POOL_HALO
    const2 = lambda i: (0, 0)
    row = lambda i: (i, 0)
    outs = [
        jax.ShapeDtypeStruct((T, d_model), F32),
        jax.ShapeDtypeStruct((T, d_model), BF16),
        jax.ShapeDtypeStruct((T, LANES), F32),
    ]
    return pl.pallas_call(
        functools.partial(_mix_kernel, ts=ts, tiles_per_seq=tiles_per_seq, d_model=d_model),
        grid=(T // ts,),
        in_specs=[
            pl.BlockSpec((ts, d_model), row),
            pl.BlockSpec((ts, d_pool), row),
            pl.BlockSpec((POOL_HALO, d_pool), lambda i: (jnp.maximum(i * halo_blocks - 1, 0), 0)),
            pl.BlockSpec((ts, 2 * d_model), row),
            pl.BlockSpec((ts, d_attn), row),
            pl.BlockSpec(pw.shape, lambda i: (0, 0, 0)),
            pl.BlockSpec((1, d_pool), const2),
            pl.BlockSpec((1, d_pool), const2),
            pl.BlockSpec(wa.shape, const2),
            pl.BlockSpec(wp.shape, const2),
            pl.BlockSpec(wo.shape, const2),
            pl.BlockSpec((1, d_model), const2),
            pl.BlockSpec(rw.shape, const2),
            pl.BlockSpec((1, LANES), const2),
        ],
        out_specs=[pl.BlockSpec((ts, s.shape[1]), row) for s in outs],
        out_shape=outs,
        compiler_params=_cparams("arbitrary"),
        name="mix",
    )(x2, u, u, sg, ya, pw, pb, ps, wa, wp, wo, g2, rw, rb)


def _router_kernel(lg_ref, lower_ref, upper_ref, lp_ref, w_ref, info_ref, carry_ref, *, ts):
    @pl.when(pl.program_id(0) == 0)
    def _():
        carry_ref[...] = jnp.zeros_like(carry_ref)

    l = lg_ref[...]
    lane = lax.broadcasted_iota(I32, (ts, LANES), 1)
    vals, hots = [], []
    for _ in range(TOP_K_EXPERTS):
        m = jnp.max(l, axis=1, keepdims=True)
        idx = jnp.min(jnp.where(l == m, lane, LANES), axis=1, keepdims=True)
        hot = lane == idx
        vals.append(m)
        hots.append(hot)
        l = jnp.where(hot, -jnp.inf, l)
    es = [jnp.exp(v - vals[0]) for v in vals]
    den = es[0] + es[1] + es[2] + es[3]
    picked = jnp.zeros((ts, LANES), F32)
    for hot in hots:
        picked = picked + jnp.where(hot, 1.0, 0.0)
    before = _dot(lower_ref[...], picked.astype(BF16))
    n = jnp.sum(picked, axis=0, keepdims=True)
    n_pad = jnp.floor((n + (ROW_ALIGN - 1)) * (1.0 / ROW_ALIGN)) * ROW_ALIGN
    run_start = _dot(jnp.broadcast_to(n_pad, (8, LANES)).astype(BF16), upper_ref[...])[:1]
    pos = before + run_start
    lp_o = jnp.zeros((ts, LANES), F32)
    w_o = jnp.zeros((ts, LANES), F32)
    for k in range(TOP_K_EXPERTS):
        slot = lane == k
        lp_o = jnp.where(slot, jnp.sum(jnp.where(hots[k], pos, 0.0), axis=1, keepdims=True), lp_o)
        w_o = jnp.where(slot, es[k] / den, w_o)
    lp_ref[...] = lp_o.astype(I32)
    w_ref[...] = w_o
    row = lax.broadcasted_iota(I32, (8, LANES), 0)
    info_ref[...] = jnp.where(row == 0, n_pad, jnp.where(row == 1, run_start, jnp.where(row == 2, carry_ref[...], 0.0)))
    carry_ref[...] = carry_ref[...] + n_pad


def _router(logits, lower, upper, *, ts):
    T = logits.shape[0]
    row = lambda i: (i, 0)
    const = lambda i: (0, 0)
    outs = [
        jax.ShapeDtypeStruct((T, LANES), I32),
        jax.ShapeDtypeStruct((T, LANES), F32),
        jax.ShapeDtypeStruct((T // ts * 8, LANES), F32),
    ]
    return pl.pallas_call(
        functools.partial(_router_kernel, ts=ts),
        grid=(T // ts,),
        in_specs=[pl.BlockSpec((ts, LANES), row), pl.BlockSpec((ts, ts), const), pl.BlockSpec((LANES, LANES), const)],
        out_specs=[pl.BlockSpec((ts, LANES), row), pl.BlockSpec((ts, LANES), row), pl.BlockSpec((8, LANES), row)],
        out_shape=outs,
        scratch_shapes=[pltpu.VMEM((1, LANES), F32)],
        compiler_params=_cparams("arbitrary"),
        name="router",
    )(logits, lower, upper)


def _run_chunks(ts):
    c, out = ts, []
    while c >= ROW_ALIGN:
        out.append(c)
        c //= 2
    return tuple(out)


def _for_each_run_chunk(i, n_ref, ls_ref, go_ref, chunks, fn):
    for e in range(N_EXPERTS):
        n = n_ref[i * N_EXPERTS + e]
        ls = ls_ref[i * N_EXPERTS + e]
        go = go_ref[i * N_EXPERTS + e]
        off = jnp.int32(0)
        for c in chunks:
            @pl.when((n & c) != 0)
            def _(off=off, c=c, ls=ls, go=go):
                fn(pl.multiple_of(ls + off, ROW_ALIGN), pl.multiple_of(go + off, ROW_ALIGN), c)
            off = off + (n & c)


def _dispatch_kernel(n_ref, ls_ref, go_ref, lpt_ref, h_ref, xs_in_ref, xs_ref, xl_ref, sem, *, ts, lp_rows):
    del xs_in_ref
    i = pl.program_id(0)
    h = h_ref[...]
    for r in range(lp_rows // PERM_ROWS):
        prow = r * PERM_ROWS + lax.broadcasted_iota(I32, (PERM_ROWS, ts), 0)
        hit = prow == lpt_ref[0:1, :]
        for k in range(1, TOP_K_EXPERTS):
            hit = hit | (prow == lpt_ref[k:k + 1, :])
        xl_ref[r * PERM_ROWS:(r + 1) * PERM_ROWS, :] = _dot(jnp.where(hit, 1.0, 0.0).astype(BF16), h)

    def copy(local_row, global_row, rows):
        return pltpu.make_async_copy(xl_ref.at[pl.ds(local_row, rows)], xs_ref.at[pl.ds(global_row, rows)], sem)

    chunks = _run_chunks(ts)
    _for_each_run_chunk(i, n_ref, ls_ref, go_ref, chunks, lambda a, b, c: copy(a, b, c).start())
    _for_each_run_chunk(i, n_ref, ls_ref, go_ref, chunks, lambda a, b, c: copy(a, b, c).wait())


def _dispatch(n_tab, ls_tab, go_tab, lpt, h2, xs_init, *, ts, lp_rows):
    T, d = h2.shape
    grid_spec = pltpu.PrefetchScalarGridSpec(
        num_scalar_prefetch=3,
        grid=(T // ts,),
        in_specs=[
            pl.BlockSpec((8, ts), lambda i, *_: (0, i)),
            pl.BlockSpec((ts, d), lambda i, *_: (i, 0)),
            pl.BlockSpec(memory_space=pl.ANY),
        ],
        out_specs=pl.BlockSpec(memory_space=pl.ANY),
        scratch_shapes=[pltpu.VMEM((lp_rows, d), F32), pltpu.SemaphoreType.DMA(())],
    )
    return pl.pallas_call(
        functools.partial(_dispatch_kernel, ts=ts, lp_rows=lp_rows),
        grid_spec=grid_spec,
        out_shape=jax.ShapeDtypeStruct(xs_init.shape, xs_init.dtype),
        input_output_aliases={5: 0},
        compiler_params=_cparams("arbitrary"),
        name="dispatch",
    )(n_tab, ls_tab, go_tab, lpt, h2, xs_init)


def _expert_kernel(te_ref, tv_ref, x_ref, w1_ref, b1_ref, w2_ref, b2_ref, y_ref, w1b_ref, w2b_ref, *, d_expert):
    i = pl.program_id(0)

    @pl.when(jnp.logical_or(i == 0, te_ref[i] != te_ref[jnp.maximum(i - 1, 0)]))
    def _():
        w1b_ref[...] = w1_ref[...].astype(BF16)
        w2b_ref[...] = w2_ref[...].astype(BF16)

    @pl.when(tv_ref[i] == 1)
    def _():
        gu = _dot(x_ref[...].astype(BF16), w1b_ref[...]) + b1_ref[...]
        gate = jnp.minimum(gu[:, :d_expert], SWIGLU_LIMIT)
        lin = jnp.clip(gu[:, d_expert:], -SWIGLU_LIMIT, SWIGLU_LIMIT)
        act = (lin + 1.0) * gate * jax.nn.sigmoid(SWIGLU_ALPHA * gate)
        y_ref[...] = _dot(act.astype(BF16), w2b_ref[...]) + b2_ref[...]

    @pl.when(tv_ref[i] == 0)
    def _():
        y_ref[...] = jnp.zeros_like(y_ref)


def _experts(tile_expert, tile_valid, xs, w1, b1, w2, b2, *, tm):
    P, d = xs.shape
    d_expert = w2.shape[1]
    grid_spec = pltpu.PrefetchScalarGridSpec(
        num_scalar_prefetch=2,
        grid=(P // tm,),
        in_specs=[
            pl.BlockSpec((tm, d), lambda i, te, tv: (i, 0)),
            pl.BlockSpec((None, d, 2 * d_expert), lambda i, te, tv: (te[i], 0, 0)),
            pl.BlockSpec((None, 1, 2 * d_expert), lambda i, te, tv: (te[i], 0, 0)),
            pl.BlockSpec((None, d_expert, d), lambda i, te, tv: (te[i], 0, 0)),
            pl.BlockSpec((None, 1, d), lambda i, te, tv: (te[i], 0, 0)),
        ],
        out_specs=pl.BlockSpec((tm, d), lambda i, te, tv: (i, 0)),
        scratch_shapes=[pltpu.VMEM((d, 2 * d_expert), BF16), pltpu.VMEM((d_expert, d), BF16)],
    )
    return pl.pallas_call(
        functools.partial(_expert_kernel, d_expert=d_expert),
        grid_spec=grid_spec,
        out_shape=jax.ShapeDtypeStruct((P, d), F32),
        compiler_params=_cparams("arbitrary"),
        name="experts",
    )(tile_expert, tile_valid, xs, w1, b1, w2, b2)


def _combine_kernel(n_ref, ls_ref, go_ref, lp_ref, w_ref, x1_ref, y_ref, o_ref, yl_ref, sem, *, ts, lp_rows):
    i = pl.program_id(0)

    @pl.when(i == 0)
    def _():
        yl_ref[...] = jnp.zeros_like(yl_ref)

    def copy(local_row, global_row, rows):
        return pltpu.make_async_copy(y_ref.at[pl.ds(global_row, rows)], yl_ref.at[pl.ds(local_row, rows)], sem)

    chunks = _run_chunks(ts)
    _for_each_run_chunk(i, n_ref, ls_ref, go_ref, chunks, lambda a, b, c: copy(a, b, c).start())
    _for_each_run_chunk(i, n_ref, ls_ref, go_ref, chunks, lambda a, b, c: copy(a, b, c).wait())

    lp = lp_ref[...]
    w = w_ref[...]
    lp_b = [jnp.broadcast_to(lp[:, k:k + 1], (ts, PERM_ROWS)) for k in range(TOP_K_EXPERTS)]
    w_b = [jnp.broadcast_to(w[:, k:k + 1], (ts, PERM_ROWS)) for k in range(TOP_K_EXPERTS)]
    out = x1_ref[...]
    for r in range(lp_rows // PERM_ROWS):
        pcol = r * PERM_ROWS + lax.broadcasted_iota(I32, (ts, PERM_ROWS), 1)
        wm = jnp.zeros((ts, PERM_ROWS), F32)
        for k in range(TOP_K_EXPERTS):
            wm = wm + jnp.where(pcol == lp_b[k], w_b[k], 0.0)
        out = out + _dot(wm.astype(BF16), yl_ref[r * PERM_ROWS:(r + 1) * PERM_ROWS, :].astype(BF16))
    o_ref[...] = out


def _combine(n_tab, ls_tab, go_tab, lp, wts, x1, y, *, ts, lp_rows):
    T, d = x1.shape
    grid_spec = pltpu.PrefetchScalarGridSpec(
        num_scalar_prefetch=3,
        grid=(T // ts,),
        in_specs=[
            pl.BlockSpec((ts, LANES), lambda i, *_: (i, 0)),
            pl.BlockSpec((ts, LANES), lambda i, *_: (i, 0)),
            pl.BlockSpec((ts, d), lambda i, *_: (i, 0)),
            pl.BlockSpec(memory_space=pl.ANY),
        ],
        out_specs=pl.BlockSpec((ts, d), lambda i, *_: (i, 0)),
        scratch_shapes=[pltpu.VMEM((lp_rows, d), F32), pltpu.SemaphoreType.DMA(())],
    )
    return pl.pallas_call(
        functools.partial(_combine_kernel, ts=ts, lp_rows=lp_rows),
        grid_spec=grid_spec,
        out_shape=jax.ShapeDtypeStruct((T, d), F32),
        compiler_params=_cparams("arbitrary"),
        name="combine",
    )(n_tab, ls_tab, go_tab, lp, wts, x1, y)


def _pick(n, prefs):
    for p in prefs:
        if n % p == 0:
            return p
    raise ValueError(f"no tile in {prefs} divides {n}")


def _layer(x, attn_norm_g, w_in, q_norm_g, k_norm_g, pool_w, pool_b, pool_scale, w_branch_attn,
           w_branch_pool, w_out, ffn_norm_g, router_w, router_b, w1, b1, w2, b2):
    B, S, d_model = x.shape
    T = B * S
    d_attn = N_HEADS * HEAD_DIM
    d_pool = len(POOL_WINDOWS) * LANES
    d_idx = N_IDX_HEADS * IDX_DIM
    assert d_attn == d_model // 2 and d_pool == d_model // 2 and d_idx == 2 * LANES
    assert w_in.shape[1] == 3 * d_attn + d_idx + IDX_DIM + N_IDX_HEADS + d_pool + 2 * d_model

    o = 3 * d_attn
    w_iq = w_in[:, o:o + d_idx]
    w_ik = w_in[:, o + d_idx:o + d_idx + IDX_DIM]
    w_iw = w_in[:, o + d_idx + IDX_DIM:o + d_idx + IDX_DIM + N_IDX_HEADS]
    o2 = o + d_idx + IDX_DIM + N_IDX_HEADS
    w_cat = jnp.concatenate([
        w_in[:, :o], w_iq,
        jnp.pad(w_ik, ((0, 0), (0, LANES - IDX_DIM))),
        jnp.pad(w_iw, ((0, 0), (0, LANES - N_IDX_HEADS))),
        w_in[:, o2:]], axis=1).astype(BF16)
    head_of = jnp.arange(d_attn) // HEAD_DIM
    bd = (head_of[:, None] == head_of[None, :]).astype(BF16)

    x2 = x.reshape(T, d_model)
    q, k, v, iq, ik, iw, u, sg = _inproj(
        x2, attn_norm_g.reshape(1, d_model), w_cat,
        jnp.tile(q_norm_g, N_HEADS).reshape(1, d_attn), jnp.tile(k_norm_g, N_HEADS).reshape(1, d_attn), bd,
        tm=_pick(T, (256, 128)), d_attn=d_attn, d_pool=d_pool, d_model=d_model)

    topk = min(TOPK_MAX, S // 4)
    tq = _pick(S, (256, 128))
    tk = _pick(S, (512, 256, 128))
    kt = jnp.swapaxes(k.reshape(B, S, d_attn), 1, 2)
    ikt = jnp.swapaxes(ik.reshape(B, S, LANES)[:, :, :IDX_DIM], 1, 2)
    ikt2 = jnp.concatenate([ikt, ikt], axis=1)
    tri = (jnp.arange(tk)[:, None] < jnp.arange(tk)[None, :]).astype(BF16)
    y_attn = _dsa(q.reshape(B, S, d_attn), iq.reshape(B, S, 2 * LANES), iw.reshape(B, S, LANES),
                  kt, v.reshape(B, S, d_attn), ikt2, tri, tq=tq, tk=tk, topk=topk)

    rw = jnp.pad(router_w, ((0, 0), (0, LANES - N_EXPERTS))).astype(BF16)
    rb = jnp.concatenate([router_b.astype(F32), jnp.full((LANES - N_EXPERTS,), NEG_BIG, F32)]).reshape(1, LANES)
    x1, h2, logits = _mix(
        x2, u, sg, y_attn.reshape(T, d_attn), pool_w.astype(BF16), pool_b.reshape(1, d_pool),
        pool_scale.reshape(1, d_pool), w_branch_attn.astype(BF16), w_branch_pool.astype(BF16),
        w_out.astype(BF16), ffn_norm_g.reshape(1, d_model), rw, rb, ts=_pick(S, (256, 128)), seq=S)

    tr = _pick(T, (512, 256, 128))
    n_tok_tiles = T // tr
    lp_rows = tr * TOP_K_EXPERTS + N_EXPERTS * ROW_ALIGN
    assert lp_rows % PERM_ROWS == 0
    lower = (jnp.arange(tr)[:, None] > jnp.arange(tr)[None, :]).astype(BF16)
    upper = (jnp.arange(LANES)[:, None] < jnp.arange(LANES)[None, :]).astype(BF16)
    lp, wts, info = _router(logits, lower, upper, ts=tr)
    info = info.reshape(n_tok_tiles, 8, LANES)[:, :, :N_EXPERTS].astype(I32)
    run_len, run_start, rows_before = info[:, 0], info[:, 1], info[:, 2]
    tm = EXPERT_ROWS
    total = rows_before[-1] + run_len[-1]
    group = (total + tm - 1) // tm * tm
    ends = jnp.cumsum(group)
    run_global = (ends - group)[None, :] + rows_before
    max_rows = T * TOP_K_EXPERTS + n_tok_tiles * N_EXPERTS * (ROW_ALIGN - 1) + N_EXPERTS * (tm - 1)
    n_tiles = (max_rows + tm - 1) // tm
    tile_row = jnp.arange(n_tiles, dtype=I32) * tm
    tile_valid = (tile_row < ends[-1]).astype(I32)
    tile_expert = jnp.minimum(jnp.sum((tile_row[:, None] >= ends[None, :]).astype(I32), axis=1), N_EXPERTS - 1)
    tabs = (run_len.reshape(-1), run_start.reshape(-1), run_global.reshape(-1))

    lpt = jnp.pad(lp[:, :TOP_K_EXPERTS].T, ((0, 8 - TOP_K_EXPERTS), (0, 0)), constant_values=-1)
    xs = _dispatch(*tabs, lpt, h2, jnp.zeros((n_tiles * tm, d_model), F32), ts=tr, lp_rows=lp_rows)
    y = _experts(tile_expert, tile_valid, xs, w1, b1.reshape(N_EXPERTS, 1, -1),
                 w2, b2.reshape(N_EXPERTS, 1, -1), tm=tm)
    out = _combine(*tabs, lp, wts, x1, y, ts=tr, lp_rows=lp_rows)
    return out.reshape(B, S, d_model)


def kernel(x, attn_norm_g, w_in, q_norm_g, k_norm_g, pool_w, pool_b, pool_scale, w_branch_attn, w_branch_pool,
           w_out, ffn_norm_g, router_w, router_b, expert_w1, expert_b1, expert_w2, expert_b2):
    for l in range(attn_norm_g.shape[0]):
        x = _layer(x, attn_norm_g[l], w_in[l], q_norm_g[l], k_norm_g[l], pool_w[l], pool_b[l], pool_scale[l],
                   w_branch_attn[l], w_branch_pool[l], w_out[l], ffn_norm_g[l], router_w[l], router_b[l],
                   expert_w1[l], expert_b1[l], expert_w2[l], expert_b2[l])
    return x
```

```python
import functools

import jax
import jax.numpy as jnp
from jax import lax
from jax.experimental import pallas as pl
from jax.experimental.pallas import tpu as pltpu

F32 = jnp.float32
BF16 = jnp.bfloat16
I32 = jnp.int32

EPS = 1e-6
CHUNK = 64
HEAD_DIM = 64
N_HEADS = 8
N_IDX_HEADS = 4
IDX_DIM = 64
TOPK_MAX = 256
POOL_WINDOWS = (2, 4, 8, 16)
POOL_HALO = 16
N_EXPERTS = 32
TOP_K_EXPERTS = 4
SWIGLU_LIMIT = 7.0
SWIGLU_ALPHA = 1.702

LANES = 128
COUNT_ROWS = 64
ROW_ALIGN = 8
PERM_ROWS = 256
EXPERT_ROWS = 512
NEG_BIG = -1e30
INT_MIN = -(2 ** 31)
KEY_NEG_FLT_MAX = INT_MIN + 0x00800000
VMEM_LIMIT = 56 * 1024 * 1024


def _cparams(*sem):
    return pltpu.CompilerParams(dimension_semantics=sem, vmem_limit_bytes=VMEM_LIMIT)


def _dot(a, b):
    return jnp.dot(a, b, preferred_element_type=F32)


def _inproj_kernel(x_ref, g_ref, w_ref, gq_ref, gk_ref, bd_ref,
                   q_ref, k_ref, v_ref, iq_ref, ik_ref, iw_ref, u_ref, sg_ref, *, d_attn, d_pool):
    x = x_ref[...]
    h = x * lax.rsqrt(jnp.mean(x * x, axis=-1, keepdims=True) + EPS) * g_ref[...]
    hb = h.astype(BF16)
    bd = bd_ref[...]

    def proj(lo, hi):
        return _dot(hb, w_ref[:, lo:hi])

    def head_norm(t, g, scale):
        t2 = t * t
        hi = t2.astype(BF16)
        lo = (t2 - hi.astype(F32)).astype(BF16)
        ss = _dot(hi, bd) + _dot(lo, bd)
        return t * lax.rsqrt(ss * (1.0 / HEAD_DIM) + EPS) * (g * scale)

    o = 0
    q_ref[...] = head_norm(proj(o, o + d_attn), gq_ref[...], HEAD_DIM ** -0.5).astype(BF16)
    o += d_attn
    k_ref[...] = head_norm(proj(o, o + d_attn), gk_ref[...], 1.0).astype(BF16)
    o += d_attn
    v_ref[...] = proj(o, o + d_attn).astype(BF16)
    o += d_attn
    idx = proj(o, o + 4 * LANES)
    iq_ref[...] = (idx[:, :2 * LANES] * (IDX_DIM ** -0.5)).astype(BF16)
    ik_ref[...] = idx[:, 2 * LANES:3 * LANES].astype(BF16)
    iw_ref[...] = idx[:, 3 * LANES:] * (N_IDX_HEADS ** -0.5)
    o += 4 * LANES
    u_ref[...] = proj(o, o + d_pool)
    o += d_pool
    sg_ref[...] = jax.nn.sigmoid(proj(o, w_ref.shape[1]))


def _inproj(x2, g, w_cat, gq, gk, bd, *, tm, d_attn, d_pool, d_model):
    T = x2.shape[0]
    n = w_cat.shape[1]
    const = lambda i: (0, 0)
    row = lambda i: (i, 0)
    outs = [
        jax.ShapeDtypeStruct((T, d_attn), BF16),
        jax.ShapeDtypeStruct((T, d_attn), BF16),
        jax.ShapeDtypeStruct((T, d_attn), BF16),
        jax.ShapeDtypeStruct((T, 2 * LANES), BF16),
        jax.ShapeDtypeStruct((T, LANES), BF16),
        jax.ShapeDtypeStruct((T, LANES), F32),
        jax.ShapeDtypeStruct((T, d_pool), F32),
        jax.ShapeDtypeStruct((T, 2 * d_model), F32),
    ]
    return pl.pallas_call(
        functools.partial(_inproj_kernel, d_attn=d_attn, d_pool=d_pool),
        grid=(T // tm,),
        in_specs=[
            pl.BlockSpec((tm, d_model), row),
            pl.BlockSpec((1, d_model), const),
            pl.BlockSpec((d_model, n), const),
            pl.BlockSpec((1, d_attn), const),
            pl.BlockSpec((1, d_attn), const),
            pl.BlockSpec((d_attn, d_attn), const),
        ],
        out_specs=[pl.BlockSpec((tm, s.shape[1]), row) for s in outs],
        out_shape=outs,
        compiler_params=_cparams("arbitrary"),
        name="inproj",
    )(x2, g, w_cat, gq, gk, bd)


def _key_to_float(k):
    bits = jnp.where(k >= 0, k, k ^ 0x7FFFFFFF)
    return lax.bitcast_convert_type(bits, F32)


def _dsa_kernel(q_ref, iq_ref, iw_ref, kt_ref, v_ref, ikt_ref, tri_ref, o_ref, sc_ref, m_ref, acc_ref,
                *, tq, tk, topk):
    qi = pl.program_id(1)
    row0 = qi * tq
    nvis = (row0 + tq + tk - 1) // tk
    rows = row0 + lax.broadcasted_iota(I32, (tq, 1), 0)
    limit = (rows // CHUNK + 1) * CHUNK
    lane = lax.broadcasted_iota(I32, (tq, LANES), 1)
    low = lane < HEAD_DIM
    n_sub = tk // LANES
    cb = 2 if (sc_ref.shape[1] // tk) % 2 == 0 else 1
    nvis_c = (nvis + cb - 1) // cb
    zero_b = jnp.zeros((), BF16)
    one_b = jnp.ones((), BF16)

    def blk(kb):
        return pl.ds(pl.multiple_of(kb * tk, tk), tk)

    def head_masked(pair):
        return [jnp.where(low, pair, zero_b), jnp.where(low, zero_b, pair)]

    iq = iq_ref[...]
    iw = iw_ref[...]
    iq_h = []
    for j in range(N_IDX_HEADS // 2):
        iq_h += head_masked(iq[:, j * LANES:(j + 1) * LANES])
    iw_b = [jnp.broadcast_to(iw[:, h:h + 1], (tq, tk)) for h in range(N_IDX_HEADS)]
    col = lax.broadcasted_iota(I32, (tq, tk), 1)

    def score_body(kb, carry):
        ikt = ikt_ref[:, blk(kb)]
        s = jnp.zeros((tq, tk), F32)
        for h in range(N_IDX_HEADS):
            s = s + iw_b[h] * jnp.maximum(_dot(iq_h[h], ikt), 0.0)
        s = jnp.where(col + kb * tk < limit, s, -jnp.inf)
        sc_ref[:, blk(kb)] = s
        return carry

    lax.fori_loop(0, nvis_c * cb, score_body, 0)

    n_rc = tq // COUNT_ROWS

    def count(cand, strict):
        cands = [jnp.broadcast_to(cand[r * COUNT_ROWS:(r + 1) * COUNT_ROWS], (COUNT_ROWS, LANES))
                 for r in range(n_rc)]

        def body(kc, accs):
            out = []
            for r in range(n_rc):
                acc = accs[r]
                for j in range(cb * n_sub):
                    s = sc_ref[r * COUNT_ROWS:(r + 1) * COUNT_ROWS,
                               pl.ds(pl.multiple_of(kc * (cb * tk) + j * LANES, LANES), LANES)]
                    hit = (s > cands[r]) if strict else (s >= cands[r])
                    acc = acc + jnp.where(hit, 1.0, 0.0)
                out.append(acc)
            return tuple(out)

        accs = lax.fori_loop(0, nvis_c, body, tuple(jnp.zeros((COUNT_ROWS, LANES), F32) for _ in range(n_rc)))
        return jnp.concatenate([jnp.sum(a, axis=1, keepdims=True) for a in accs], axis=0)

    kf = float(topk)
    cnt0 = count(jnp.zeros((tq, 1), F32), False)
    ok0 = cnt0 >= kf
    c0 = jnp.where(ok0, 0, INT_MIN).astype(I32)
    n0 = jnp.where(ok0, cnt0, 0.0)

    def search_body(i, st):
        c, n_ge = st
        cand = c | jnp.left_shift(jnp.int32(1), 30 - i)
        cnt = count(_key_to_float(cand), False)
        ok = cnt >= kf
        return jnp.where(ok, cand, c), jnp.where(ok, cnt, n_ge)

    ckey, n_ge = lax.fori_loop(0, 31, search_body, (c0, n0))
    thr = _key_to_float(jnp.maximum(ckey, KEY_NEG_FLT_MAX))
    need = kf - count(thr, True)
    ambiguous = jnp.max(jnp.where(n_ge > kf, 1.0, 0.0)) > 0.0

    @pl.when(jnp.logical_not(ambiguous))
    def _():
        def body(kb, carry):
            sc_ref[:, blk(kb)] = jnp.where(sc_ref[:, blk(kb)] >= thr, 0.0, NEG_BIG)
            return carry
        lax.fori_loop(0, nvis, body, 0)

    @pl.when(ambiguous)
    def _():
        tri = tri_ref[...]

        def body(kb, carry):
            s = sc_ref[:, blk(kb)]
            eq = s == thr
            eqf = jnp.where(eq, 1.0, 0.0)
            before = carry + _dot(eqf.astype(BF16), tri)
            sel = (s > thr) | (eq & (before < need))
            sc_ref[:, blk(kb)] = jnp.where(sel, 0.0, NEG_BIG)
            return carry + jnp.sum(eqf, axis=1, keepdims=True)
        lax.fori_loop(0, nvis, body, jnp.zeros((tq, 1), F32))

    q = q_ref[...]
    low_k = lax.broadcasted_iota(I32, (tk, LANES), 1) < HEAD_DIM
    for j in range(N_HEADS // 2):
        qm = head_masked(q[:, j * LANES:(j + 1) * LANES])
        m_ref[...] = jnp.full((2, tq, LANES), NEG_BIG, F32)
        acc_ref[...] = jnp.zeros((2, tq, LANES), F32)

        def logits(kb, j=j, qm=qm):
            kt = kt_ref[j * LANES:(j + 1) * LANES, blk(kb)]
            return tuple(_dot(qm[e], kt) for e in range(2))

        def body(kb, lgs, j=j):
            nxt = logits(jnp.minimum(kb + 1, nvis - 1))
            vv = v_ref[blk(kb), j * LANES:(j + 1) * LANES]
            bias = sc_ref[:, blk(kb)]
            for e in range(2):
                vh = jnp.where(low_k if e == 0 else jnp.logical_not(low_k), vv, one_b)
                lg = lgs[e] + bias
                m_old = m_ref[e]
                m_new = jnp.maximum(m_old, jnp.max(lg, axis=1, keepdims=True))
                p = jnp.concatenate(
                    [jnp.exp(lg[:, c * LANES:(c + 1) * LANES] - m_new) for c in range(n_sub)], axis=1)
                acc_ref[e] = jnp.exp(m_old - m_new) * acc_ref[e] + _dot(p.astype(BF16), vh)
                m_ref[e] = m_new
            return nxt

        lax.fori_loop(0, nvis, body, logits(0))
        outs = []
        for e in range(2):
            acc = acc_ref[e]
            outs.append(acc / pltpu.roll(acc, HEAD_DIM, 1))
        o_ref[:, j * LANES:(j + 1) * LANES] = jnp.where(low, outs[0], outs[1]).astype(BF16)


def _dsa(q, iq, iw, kt, v, ikt, tri, *, tq, tk, topk):
    B, S, d_attn = q.shape
    once = dict(pipeline_mode=pl.Buffered(1))
    return pl.pallas_call(
        functools.partial(_dsa_kernel, tq=tq, tk=tk, topk=topk),
        grid=(B, S // tq),
        in_specs=[
            pl.BlockSpec((None, tq, d_attn), lambda b, i: (b, i, 0)),
            pl.BlockSpec((None, tq, 2 * LANES), lambda b, i: (b, i, 0)),
            pl.BlockSpec((None, tq, LANES), lambda b, i: (b, i, 0)),
            pl.BlockSpec((None, d_attn, S), lambda b, i: (b, 0, 0), **once),
            pl.BlockSpec((None, S, d_attn), lambda b, i: (b, 0, 0), **once),
            pl.BlockSpec((None, LANES, S), lambda b, i: (b, 0, 0), **once),
            pl.BlockSpec((tk, tk), lambda b, i: (0, 0), **once),
        ],
        out_specs=pl.BlockSpec((None, tq, d_attn), lambda b, i: (b, i, 0)),
        out_shape=jax.ShapeDtypeStruct((B, S, d_attn), BF16),
        scratch_shapes=[pltpu.VMEM((tq, S), F32), pltpu.VMEM((2, tq, LANES), F32), pltpu.VMEM((2, tq, LANES), F32)],
        compiler_params=_cparams("arbitrary", "arbitrary"),
        name="dsa",
    )(q, iq, iw, kt, v, ikt, tri)


def _mix_kernel(x_ref, u_ref, uh_ref, sg_ref, ya_ref, pw_ref, pb_ref, ps_ref, wa_ref, wp_ref, wo_ref,
                g2_ref, rw_ref, rb_ref, x1_ref, h2_ref, lg_ref, *, ts, tiles_per_seq, d_model):
    it = pl.program_id(0) % tiles_per_seq
    u = u_ref[...]
    halo = jnp.where(it == 0, 0.0, uh_ref[...])
    ext = jnp.concatenate([halo, u], axis=0)
    t_in_seq = it * ts + lax.broadcasted_iota(I32, (ts, 1), 0)
    ys = []
    for g, w in enumerate(POOL_WINDOWS):
        s = ext[:, g * LANES:(g + 1) * LANES]
        d = 1
        while d < w:
            s = s + pltpu.roll(s, d, 0)
            d *= 2
        cnt = jnp.minimum(t_in_seq + 1, w).astype(F32)
        y = s[POOL_HALO:, :] / cnt - u[:, g * LANES:(g + 1) * LANES]
        ys.append(_dot(y.astype(BF16), pw_ref[g]))
    y_pool = (jnp.concatenate(ys, axis=1) + pb_ref[...]) * ps_ref[...]
    sg = sg_ref[...]
    merged = (sg[:, :d_model] * _dot(ya_ref[...], wa_ref[...])
              + sg[:, d_model:] * _dot(y_pool.astype(BF16), wp_ref[...]))
    x1 = x_ref[...] + _dot(merged.astype(BF16), wo_ref[...])
    x1_ref[...] = x1
    h2 = x1 * lax.rsqrt(jnp.mean(x1 * x1, axis=-1, keepdims=True) + EPS) * g2_ref[...]
    h2b = h2.astype(BF16)
    h2_ref[...] = h2b
    lg_ref[...] = _dot(h2b, rw_ref[...]) + rb_ref[...]


def _mix(x2, u, sg, ya, pw, pb, ps, wa, wp, wo, g2, rw, rb, *, ts, seq):
    T, d_model = x2.shape
    d_pool = u.shape[1]
    d_attn = ya.shape[1]
    tiles_per_seq = seq // ts
    halo_blocks = ts // POOL_HALO
    const2 = lambda i: (0, 0)
    row = lambda i: (i, 0)
    outs = [
        jax.ShapeDtypeStruct((T, d_model), F32),
        jax.ShapeDtypeStruct((T, d_model), BF16),
        jax.ShapeDtypeStruct((T, LANES), F32),
    ]
    return pl.pallas_call(
        functools.partial(_mix_kernel, ts=ts, tiles_per_seq=tiles_per_seq, d_model=d_model),
        grid=(T // ts,),
        in_specs=[
            pl.BlockSpec((ts, d_model), row),
            pl.BlockSpec((ts, d_pool), row),
            pl.BlockSpec((POOL_HALO, d_pool), lambda i: (jnp.maximum(i * halo_blocks - 1, 0), 0)),
            pl.BlockSpec((ts, 2 * d_model), row),
            pl.BlockSpec((ts, d_attn), row),
            pl.BlockSpec(pw.shape, lambda i: (0, 0, 0)),
            pl.BlockSpec((1, d_pool), const2),
            pl.BlockSpec((1, d_pool), const2),
            pl.BlockSpec(wa.shape, const2),
            pl.BlockSpec(wp.shape, const2),
            pl.BlockSpec(wo.shape, const2),
            pl.BlockSpec((1, d_model), const2),
            pl.BlockSpec(rw.shape, const2),
            pl.BlockSpec((1, LANES), const2),
        ],
        out_specs=[pl.BlockSpec((ts, s.shape[1]), row) for s in outs],
        out_shape=outs,
        compiler_params=_cparams("arbitrary"),
        name="mix",
    )(x2, u, u, sg, ya, pw, pb, ps, wa, wp, wo, g2, rw, rb)


def _router_kernel(lg_ref, lower_ref, upper_ref, lp_ref, w_ref, info_ref, carry_ref, *, ts):
    @pl.when(pl.program_id(0) == 0)
    def _():
        carry_ref[...] = jnp.zeros_like(carry_ref)

    l = lg_ref[...]
    lane = lax.broadcasted_iota(I32, (ts, LANES), 1)
    vals, hots = [], []
    for _ in range(TOP_K_EXPERTS):
        m = jnp.max(l, axis=1, keepdims=True)
        idx = jnp.min(jnp.where(l == m, lane, LANES), axis=1, keepdims=True)
        hot = lane == idx
        vals.append(m)
        hots.append(hot)
        l = jnp.where(hot, -jnp.inf, l)
    es = [jnp.exp(v - vals[0]) for v in vals]
    den = es[0] + es[1] + es[2] + es[3]
    picked = jnp.zeros((ts, LANES), F32)
    for hot in hots:
        picked = picked + jnp.where(hot, 1.0, 0.0)
    before = _dot(lower_ref[...], picked.astype(BF16))
    n = jnp.sum(picked, axis=0, keepdims=True)
    n_pad = jnp.floor((n + (ROW_ALIGN - 1)) * (1.0 / ROW_ALIGN)) * ROW_ALIGN
    run_start = _dot(jnp.broadcast_to(n_pad, (8, LANES)).astype(BF16), upper_ref[...])[:1]
    pos = before + run_start
    lp_o = jnp.zeros((ts, LANES), F32)
    w_o = jnp.zeros((ts, LANES), F32)
    for k in range(TOP_K_EXPERTS):
        slot = lane == k
        lp_o = jnp.where(slot, jnp.sum(jnp.where(hots[k], pos, 0.0), axis=1, keepdims=True), lp_o)
        w_o = jnp.where(slot, es[k] / den, w_o)
    lp_ref[...] = lp_o.astype(I32)
    w_ref[...] = w_o
    row = lax.broadcasted_iota(I32, (8, LANES), 0)
    info_ref[...] = jnp.where(row == 0, n_pad, jnp.where(row == 1, run_start, jnp.where(row == 2, carry_ref[...], 0.0)))
    carry_ref[...] = carry_ref[...] + n_pad


def _router(logits, lower, upper, *, ts):
    T = logits.shape[0]
    row = lambda i: (i, 0)
    const = lambda i: (0, 0)
    outs = [
        jax.ShapeDtypeStruct((T, LANES), I32),
        jax.ShapeDtypeStruct((T, LANES), F32),
        jax.ShapeDtypeStruct((T // ts * 8, LANES), F32),
    ]
    return pl.pallas_call(
        functools.partial(_router_kernel, ts=ts),
        grid=(T // ts,),
        in_specs=[pl.BlockSpec((ts, LANES), row), pl.BlockSpec((ts, ts), const), pl.BlockSpec((LANES, LANES), const)],
        out_specs=[pl.BlockSpec((ts, LANES), row), pl.BlockSpec((ts, LANES), row), pl.BlockSpec((8, LANES), row)],
        out_shape=outs,
        scratch_shapes=[pltpu.VMEM((1, LANES), F32)],
        compiler_params=_cparams("arbitrary"),
        name="router",
    )(logits, lower, upper)


def _run_chunks(ts):
    c, out = ts, []
    while c >= ROW_ALIGN:
        out.append(c)
        c //= 2
    return tuple(out)


def _for_each_run_chunk(i, n_ref, ls_ref, go_ref, chunks, fn):
    for e in range(N_EXPERTS):
        n = n_ref[i * N_EXPERTS + e]
        ls = ls_ref[i * N_EXPERTS + e]
        go = go_ref[i * N_EXPERTS + e]
        off = jnp.int32(0)
        for c in chunks:
            @pl.when((n & c) != 0)
            def _(off=off, c=c, ls=ls, go=go):
                fn(pl.multiple_of(ls + off, ROW_ALIGN), pl.multiple_of(go + off, ROW_ALIGN), c)
            off = off + (n & c)


def _dispatch_kernel(n_ref, ls_ref, go_ref, lpt_ref, h_ref, xs_in_ref, xs_ref, xl_ref, sem, *, ts, lp_rows):
    del xs_in_ref
    i = pl.program_id(0)
    h = h_ref[...]
    for r in range(lp_rows // PERM_ROWS):
        prow = r * PERM_ROWS + lax.broadcasted_iota(I32, (PERM_ROWS, ts), 0)
        hit = prow == lpt_ref[0:1, :]
        for k in range(1, TOP_K_EXPERTS):
            hit = hit | (prow == lpt_ref[k:k + 1, :])
        xl_ref[r * PERM_ROWS:(r + 1) * PERM_ROWS, :] = _dot(jnp.where(hit, 1.0, 0.0).astype(BF16), h)

    def copy(local_row, global_row, rows):
        return pltpu.make_async_copy(xl_ref.at[pl.ds(local_row, rows)], xs_ref.at[pl.ds(global_row, rows)], sem)

    chunks = _run_chunks(ts)
    _for_each_run_chunk(i, n_ref, ls_ref, go_ref, chunks, lambda a, b, c: copy(a, b, c).start())
    _for_each_run_chunk(i, n_ref, ls_ref, go_ref, chunks, lambda a, b, c: copy(a, b, c).wait())


def _dispatch(n_tab, ls_tab, go_tab, lpt, h2, xs_init, *, ts, lp_rows):
    T, d = h2.shape
    grid_spec = pltpu.PrefetchScalarGridSpec(
        num_scalar_prefetch=3,
        grid=(T // ts,),
        in_specs=[
            pl.BlockSpec((8, ts), lambda i, *_: (0, i)),
            pl.BlockSpec((ts, d), lambda i, *_: (i, 0)),
            pl.BlockSpec(memory_space=pl.ANY),
        ],
        out_specs=pl.BlockSpec(memory_space=pl.ANY),
        scratch_shapes=[pltpu.VMEM((lp_rows, d), F32), pltpu.SemaphoreType.DMA(())],
    )
    return pl.pallas_call(
        functools.partial(_dispatch_kernel, ts=ts, lp_rows=lp_rows),
        grid_spec=grid_spec,
        out_shape=jax.ShapeDtypeStruct(xs_init.shape, xs_init.dtype),
        input_output_aliases={5: 0},
        compiler_params=_cparams("arbitrary"),
        name="dispatch",
    )(n_tab, ls_tab, go_tab, lpt, h2, xs_init)


def _expert_kernel(te_ref, tv_ref, x_ref, w1_ref, b1_ref, w2_ref, b2_ref, y_ref, w1b_ref, w2b_ref, *, d_expert):
    i = pl.program_id(0)

    @pl.when(jnp.logical_or(i == 0, te_ref[i] != te_ref[jnp.maximum(i - 1, 0)]))
    def _():
        w1b_ref[...] = w1_ref[...].astype(BF16)
        w2b_ref[...] = w2_ref[...].astype(BF16)

    @pl.when(tv_ref[i] == 1)
    def _():
        gu = _dot(x_ref[...].astype(BF16), w1b_ref[...]) + b1_ref[...]
        gate = jnp.minimum(gu[:, :d_expert], SWIGLU_LIMIT)
        lin = jnp.clip(gu[:, d_expert:], -SWIGLU_LIMIT, SWIGLU_LIMIT)
        act = (lin + 1.0) * gate * jax.nn.sigmoid(SWIGLU_ALPHA * gate)
        y_ref[...] = _dot(act.astype(BF16), w2b_ref[...]) + b2_ref[...]

    @pl.when(tv_ref[i] == 0)
    def _():
        y_ref[...] = jnp.zeros_like(y_ref)


def _experts(tile_expert, tile_valid, xs, w1, b1, w2, b2, *, tm):
    P, d = xs.shape
    d_expert = w2.shape[1]
    grid_spec = pltpu.PrefetchScalarGridSpec(
        num_scalar_prefetch=2,
        grid=(P // tm,),
        in_specs=[
            pl.BlockSpec((tm, d), lambda i, te, tv: (i, 0)),
            pl.BlockSpec((None, d, 2 * d_expert), lambda i, te, tv: (te[i], 0, 0)),
            pl.BlockSpec((None, 1, 2 * d_expert), lambda i, te, tv: (te[i], 0, 0)),
            pl.BlockSpec((None, d_expert, d), lambda i, te, tv: (te[i], 0, 0)),
            pl.BlockSpec((None, 1, d), lambda i, te, tv: (te[i], 0, 0)),
        ],
        out_specs=pl.BlockSpec((tm, d), lambda i, te, tv: (i, 0)),
        scratch_shapes=[pltpu.VMEM((d, 2 * d_expert), BF16), pltpu.VMEM((d_expert, d), BF16)],
    )
    return pl.pallas_call(
        functools.partial(_expert_kernel, d_expert=d_expert),
        grid_spec=grid_spec,
        out_shape=jax.ShapeDtypeStruct((P, d), F32),
        compiler_params=_cparams("arbitrary"),
        name="experts",
    )(tile_expert, tile_valid, xs, w1, b1, w2, b2)


def _combine_kernel(n_ref, ls_ref, go_ref, lp_ref, w_ref, x1_ref, y_ref, o_ref, yl_ref, sem, *, ts, lp_rows):
    i = pl.program_id(0)

    @pl.when(i == 0)
    def _():
        yl_ref[...] = jnp.zeros_like(yl_ref)

    def copy(local_row, global_row, rows):
        return pltpu.make_async_copy(y_ref.at[pl.ds(global_row, rows)], yl_ref.at[pl.ds(local_row, rows)], sem)

    chunks = _run_chunks(ts)
    _for_each_run_chunk(i, n_ref, ls_ref, go_ref, chunks, lambda a, b, c: copy(a, b, c).start())
    _for_each_run_chunk(i, n_ref, ls_ref, go_ref, chunks, lambda a, b, c: copy(a, b, c).wait())

    lp = lp_ref[...]
    w = w_ref[...]
    lp_b = [jnp.broadcast_to(lp[:, k:k + 1], (ts, PERM_ROWS)) for k in range(TOP_K_EXPERTS)]
    w_b = [jnp.broadcast_to(w[:, k:k + 1], (ts, PERM_ROWS)) for k in range(TOP_K_EXPERTS)]
    out = x1_ref[...]
    for r in range(lp_rows // PERM_ROWS):
        pcol = r * PERM_ROWS + lax.broadcasted_iota(I32, (ts, PERM_ROWS), 1)
        wm = jnp.zeros((ts, PERM_ROWS), F32)
        for k in range(TOP_K_EXPERTS):
            wm = wm + jnp.where(pcol == lp_b[k], w_b[k], 0.0)
        out = out + _dot(wm.astype(BF16), yl_ref[r * PERM_ROWS:(r + 1) * PERM_ROWS, :].astype(BF16))
    o_ref[...] = out


def _combine(n_tab, ls_tab, go_tab, lp, wts, x1, y, *, ts, lp_rows):
    T, d = x1.shape
    grid_spec = pltpu.PrefetchScalarGridSpec(
        num_scalar_prefetch=3,
        grid=(T // ts,),
        in_specs=[
            pl.BlockSpec((ts, LANES), lambda i, *_: (i, 0)),
            pl.BlockSpec((ts, LANES), lambda i, *_: (i, 0)),
            pl.BlockSpec((ts, d), lambda i, *_: (i, 0)),
            pl.BlockSpec(memory_space=pl.ANY),
        ],
        out_specs=pl.BlockSpec((ts, d), lambda i, *_: (i, 0)),
        scratch_shapes=[pltpu.VMEM((lp_rows, d), F32), pltpu.SemaphoreType.DMA(())],
    )
    return pl.pallas_call(
        functools.partial(_combine_kernel, ts=ts, lp_rows=lp_rows),
        grid_spec=grid_spec,
        out_shape=jax.ShapeDtypeStruct((T, d), F32),
        compiler_params=_cparams("arbitrary"),
        name="combine",
    )(n_tab, ls_tab, go_tab, lp, wts, x1, y)


def _pick(n, prefs):
    for p in prefs:
        if n % p == 0:
            return p
    raise ValueError(f"no tile in {prefs} divides {n}")


def _layer(x, attn_norm_g, w_in, q_norm_g, k_norm_g, pool_w, pool_b, pool_scale, w_branch_attn,
           w_branch_pool, w_out, ffn_norm_g, router_w, router_b, w1, b1, w2, b2):
    B, S, d_model = x.shape
    T = B * S
    d_attn = N_HEADS * HEAD_DIM
    d_pool = len(POOL_WINDOWS) * LANES
    d_idx = N_IDX_HEADS * IDX_DIM
    assert d_attn == d_model // 2 and d_pool == d_model // 2 and d_idx == 2 * LANES
    assert w_in.shape[1] == 3 * d_attn + d_idx + IDX_DIM + N_IDX_HEADS + d_pool + 2 * d_model

    o = 3 * d_attn
    w_iq = w_in[:, o:o + d_idx]
    w_ik = w_in[:, o + d_idx:o + d_idx + IDX_DIM]
    w_iw = w_in[:, o + d_idx + IDX_DIM:o + d_idx + IDX_DIM + N_IDX_HEADS]
    o2 = o + d_idx + IDX_DIM + N_IDX_HEADS
    w_cat = jnp.concatenate([
        w_in[:, :o], w_iq,
        jnp.pad(w_ik, ((0, 0), (0, LANES - IDX_DIM))),
        jnp.pad(w_iw, ((0, 0), (0, LANES - N_IDX_HEADS))),
        w_in[:, o2:]], axis=1).astype(BF16)
    head_of = jnp.arange(d_attn) // HEAD_DIM
    bd = (head_of[:, None] == head_of[None, :]).astype(BF16)

    x2 = x.reshape(T, d_model)
    q, k, v, iq, ik, iw, u, sg = _inproj(
        x2, attn_norm_g.reshape(1, d_model), w_cat,
        jnp.tile(q_norm_g, N_HEADS).reshape(1, d_attn), jnp.tile(k_norm_g, N_HEADS).reshape(1, d_attn), bd,
        tm=_pick(T, (256, 128)), d_attn=d_attn, d_pool=d_pool, d_model=d_model)

    topk = min(TOPK_MAX, S // 4)
    tq = _pick(S, (512, 256, 128))
    tk = _pick(S, (512, 256, 128))
    kt = jnp.swapaxes(k.reshape(B, S, d_attn), 1, 2)
    ikt = jnp.swapaxes(ik.reshape(B, S, LANES)[:, :, :IDX_DIM], 1, 2)
    ikt2 = jnp.concatenate([ikt, ikt], axis=1)
    tri = (jnp.arange(tk)[:, None] < jnp.arange(tk)[None, :]).astype(BF16)
    y_attn = _dsa(q.reshape(B, S, d_attn), iq.reshape(B, S, 2 * LANES), iw.reshape(B, S, LANES),
                  kt, v.reshape(B, S, d_attn), ikt2, tri, tq=tq, tk=tk, topk=topk)

    rw = jnp.pad(router_w, ((0, 0), (0, LANES - N_EXPERTS))).astype(BF16)
    rb = jnp.concatenate([router_b.astype(F32), jnp.full((LANES - N_EXPERTS,), NEG_BIG, F32)]).reshape(1, LANES)
    x1, h2, logits = _mix(
        x2, u, sg, y_attn.reshape(T, d_attn), pool_w.astype(BF16), pool_b.reshape(1, d_pool),
        pool_scale.reshape(1, d_pool), w_branch_attn.astype(BF16), w_branch_pool.astype(BF16),
        w_out.astype(BF16), ffn_norm_g.reshape(1, d_model), rw, rb, ts=_pick(S, (256, 128)), seq=S)

    tr = _pick(T, (512, 256, 128))
    n_tok_tiles = T // tr
    lp_rows = tr * TOP_K_EXPERTS + N_EXPERTS * ROW_ALIGN
    assert lp_rows % PERM_ROWS == 0
    lower = (jnp.arange(tr)[:, None] > jnp.arange(tr)[None, :]).astype(BF16)
    upper = (jnp.arange(LANES)[:, None] < jnp.arange(LANES)[None, :]).astype(BF16)
    lp, wts, info = _router(logits, lower, upper, ts=tr)
    info = info.reshape(n_tok_tiles, 8, LANES)[:, :, :N_EXPERTS].astype(I32)
    run_len, run_start, rows_before = info[:, 0], info[:, 1], info[:, 2]
    tm = EXPERT_ROWS
    total = rows_before[-1] + run_len[-1]
    group = (total + tm - 1) // tm * tm
    ends = jnp.cumsum(group)
    run_global = (ends - group)[None, :] + rows_before
    max_rows = T * TOP_K_EXPERTS + n_tok_tiles * N_EXPERTS * (ROW_ALIGN - 1) + N_EXPERTS * (tm - 1)
    n_tiles = (max_rows + tm - 1) // tm
    tile_row = jnp.arange(n_tiles, dtype=I32) * tm
    tile_valid = (tile_row < ends[-1]).astype(I32)
    tile_expert = jnp.minimum(jnp.sum((tile_row[:, None] >= ends[None, :]).astype(I32), axis=1), N_EXPERTS - 1)
    tabs = (run_len.reshape(-1), run_start.reshape(-1), run_global.reshape(-1))

    lpt = jnp.pad(lp[:, :TOP_K_EXPERTS].T, ((0, 8 - TOP_K_EXPERTS), (0, 0)), constant_values=-1)
    xs = _dispatch(*tabs, lpt, h2, jnp.zeros((n_tiles * tm, d_model), F32), ts=tr, lp_rows=lp_rows)
    y = _experts(tile_expert, tile_valid, xs, w1, b1.reshape(N_EXPERTS, 1, -1),
                 w2, b2.reshape(N_EXPERTS, 1, -1), tm=tm)
    out = _combine(*tabs, lp, wts, x1, y, ts=tr, lp_rows=lp_rows)
    return out.reshape(B, S, d_model)


def kernel(x, attn_norm_g, w_in, q_norm_g, k_norm_g, pool_w, pool_b, pool_scale, w_branch_attn, w_branch_pool,
           w_out, ffn_norm_g, router_w, router_b, expert_w1, expert_b1, expert_w2, expert_b2):
    for l in range(attn_norm_g.shape[0]):
        x = _layer(x, attn_norm_g[l], w_in[l], q_norm_g[l], k_norm_g[l], pool_w[l], pool_b[l], pool_scale[l],
                   w_branch_attn[l], w_branch_pool[l], w_out[l], ffn_norm_g[l], router_w[l], router_b[l],
                   expert_w1[l], expert_b1[l], expert_w2[l], expert_b2[l])
    return x
```

```python
import functools

import jax
import jax.numpy as jnp
from jax import lax
from jax.experimental import pallas as pl
from jax.experimental.pallas import tpu as pltpu

F32 = jnp.float32
BF16 = jnp.bfloat16
I32 = jnp.int32

EPS = 1e-6
CHUNK = 64
HEAD_DIM = 64
N_HEADS = 8
N_IDX_HEADS = 4
IDX_DIM = 64
TOPK_MAX = 256
POOL_WINDOWS = (2, 4, 8, 16)
POOL_HALO = 16
N_EXPERTS = 32
TOP_K_EXPERTS = 4
SWIGLU_LIMIT = 7.0
SWIGLU_ALPHA = 1.702

LANES = 128
COUNT_ROWS = 64
ATTN_GROUP = 8
ROW_ALIGN = 8
PERM_ROWS = 256
EXPERT_ROWS = 512
NEG_BIG = -1e30
INT_MIN = -(2 ** 31)
KEY_NEG_FLT_MAX = INT_MIN + 0x00800000
VMEM_LIMIT = 56 * 1024 * 1024


def _cparams(*sem):
    return pltpu.CompilerParams(dimension_semantics=sem, vmem_limit_bytes=VMEM_LIMIT)


def _dot(a, b):
    return jnp.dot(a, b, preferred_element_type=F32)


def _inproj_kernel(x_ref, g_ref, w_ref, gq_ref, gk_ref, bd_ref,
                   q_ref, kt_ref, v_ref, iq_ref, ikt_ref, iw_ref, u_ref, sg_ref, *, d_attn, d_pool):
    x = x_ref[...]
    h = x * lax.rsqrt(jnp.mean(x * x, axis=-1, keepdims=True) + EPS) * g_ref[...]
    hb = h.astype(BF16)
    bd = bd_ref[...]

    def proj(lo, hi):
        return _dot(hb, w_ref[:, lo:hi])

    def head_norm(t, g, scale):
        t2 = t * t
        hi = t2.astype(BF16)
        lo = (t2 - hi.astype(F32)).astype(BF16)
        ss = _dot(hi, bd) + _dot(lo, bd)
        return t * lax.rsqrt(ss * (1.0 / HEAD_DIM) + EPS) * (g * scale)

    o = 0
    q_ref[...] = head_norm(proj(o, o + d_attn), gq_ref[...], HEAD_DIM ** -0.5).astype(BF16)
    o += d_attn
    kt_ref[...] = head_norm(proj(o, o + d_attn), gk_ref[...], 1.0).T.astype(BF16)
    o += d_attn
    v_ref[...] = proj(o, o + d_attn).astype(BF16)
    o += d_attn
    idx = proj(o, o + 4 * LANES)
    iq_ref[...] = (idx[:, :2 * LANES] * (IDX_DIM ** -0.5)).astype(BF16)
    ikt = idx[:, 2 * LANES:3 * LANES].T
    ikt_ref[...] = (ikt + pltpu.roll(ikt, IDX_DIM, 0)).astype(BF16)
    iw_ref[...] = idx[:, 3 * LANES:] * (N_IDX_HEADS ** -0.5)
    o += 4 * LANES
    u_ref[...] = proj(o, o + d_pool)
    o += d_pool
    sg_ref[...] = jax.nn.sigmoid(proj(o, w_ref.shape[1]))


def _inproj(x2, g, w_cat, gq, gk, bd, *, tm, d_attn, d_pool, d_model):
    T = x2.shape[0]
    n = w_cat.shape[1]
    const = lambda i: (0, 0)
    row = lambda i: (i, 0)
    outs = [
        jax.ShapeDtypeStruct((T, d_attn), BF16),
        jax.ShapeDtypeStruct((d_attn, T), BF16),
        jax.ShapeDtypeStruct((T, d_attn), BF16),
        jax.ShapeDtypeStruct((T, 2 * LANES), BF16),
        jax.ShapeDtypeStruct((LANES, T), BF16),
        jax.ShapeDtypeStruct((T, LANES), F32),
        jax.ShapeDtypeStruct((T, d_pool), F32),
        jax.ShapeDtypeStruct((T, 2 * d_model), F32),
    ]
    return pl.pallas_call(
        functools.partial(_inproj_kernel, d_attn=d_attn, d_pool=d_pool),
        grid=(T // tm,),
        in_specs=[
            pl.BlockSpec((tm, d_model), row),
            pl.BlockSpec((1, d_model), const),
            pl.BlockSpec((d_model, n), const),
            pl.BlockSpec((1, d_attn), const),
            pl.BlockSpec((1, d_attn), const),
            pl.BlockSpec((d_attn, d_attn), const),
        ],
        out_specs=[pl.BlockSpec((s.shape[0], tm), lambda i: (0, i)) if n_out in (1, 4)
                   else pl.BlockSpec((tm, s.shape[1]), row) for n_out, s in enumerate(outs)],
        out_shape=outs,
        compiler_params=_cparams("arbitrary"),
        name="inproj",
    )(x2, g, w_cat, gq, gk, bd)


def _key_to_float(k):
    bits = jnp.where(k >= 0, k, k ^ 0x7FFFFFFF)
    return lax.bitcast_convert_type(bits, F32)


def _dsa_kernel(q_ref, iq_ref, iw_ref, kt_ref, v_ref, ikt_ref, tri_ref, o_ref, sc_ref, m_ref, acc_ref,
                *, tq, tk, topk):
    qi = pl.program_id(1)
    row0 = qi * tq
    nvis = (row0 + tq + tk - 1) // tk
    rows = row0 + lax.broadcasted_iota(I32, (tq, 1), 0)
    limit = (rows // CHUNK + 1) * CHUNK
    lane = lax.broadcasted_iota(I32, (tq, LANES), 1)
    low = lane < HEAD_DIM
    n_sub = tk // LANES
    cb = 2 if (sc_ref.shape[1] // tk) % 2 == 0 else 1
    nvis_c = (nvis + cb - 1) // cb
    zero_b = jnp.zeros((), BF16)
    one_b = jnp.ones((), BF16)

    def blk(kb):
        return pl.ds(pl.multiple_of(kb * tk, tk), tk)

    def head_masked(pair):
        return [jnp.where(low, pair, zero_b), jnp.where(low, zero_b, pair)]

    iq = iq_ref[...]
    iw = iw_ref[...]
    iq_h = []
    for j in range(N_IDX_HEADS // 2):
        iq_h += head_masked(iq[:, j * LANES:(j + 1) * LANES])
    iw_b = [jnp.broadcast_to(iw[:, h:h + 1], (tq, tk)) for h in range(N_IDX_HEADS)]
    col = lax.broadcasted_iota(I32, (tq, tk), 1)

    def score_body(kb, carry):
        ikt = ikt_ref[:, blk(kb)]
        s = jnp.zeros((tq, tk), F32)
        for h in range(N_IDX_HEADS):
            s = s + iw_b[h] * jnp.maximum(_dot(iq_h[h], ikt), 0.0)
        s = jnp.where(col + kb * tk < limit, s, -jnp.inf)
        sc_ref[:, blk(kb)] = s
        return carry

    lax.fori_loop(0, nvis_c * cb, score_body, 0)

    n_rc = tq // COUNT_ROWS

    def count(cand, strict):
        cands = [jnp.broadcast_to(cand[r * COUNT_ROWS:(r + 1) * COUNT_ROWS], (COUNT_ROWS, LANES))
                 for r in range(n_rc)]

        def body(kc, accs):
            out = []
            for r in range(n_rc):
                acc = accs[r]
                for j in range(cb * n_sub):
                    s = sc_ref[r * COUNT_ROWS:(r + 1) * COUNT_ROWS,
                               pl.ds(pl.multiple_of(kc * (cb * tk) + j * LANES, LANES), LANES)]
                    hit = (s > cands[r]) if strict else (s >= cands[r])
                    acc = acc + jnp.where(hit, 1.0, 0.0)
                out.append(acc)
            return tuple(out)

        accs = lax.fori_loop(0, nvis_c, body, tuple(jnp.zeros((COUNT_ROWS, LANES), F32) for _ in range(n_rc)))
        return jnp.concatenate([jnp.sum(a, axis=1, keepdims=True) for a in accs], axis=0)

    kf = float(topk)
    cnt0 = count(jnp.zeros((tq, 1), F32), False)
    ok0 = cnt0 >= kf
    c0 = jnp.where(ok0, 0, INT_MIN).astype(I32)
    n0 = jnp.where(ok0, cnt0, 0.0)

    def search_body(i, st):
        c, n_ge = st
        cand = c | jnp.left_shift(jnp.int32(1), 30 - i)
        cnt = count(_key_to_float(cand), False)
        ok = cnt >= kf
        return jnp.where(ok, cand, c), jnp.where(ok, cnt, n_ge)

    ckey, n_ge = lax.fori_loop(0, 31, search_body, (c0, n0))
    thr = _key_to_float(jnp.maximum(ckey, KEY_NEG_FLT_MAX))
    need = kf - count(thr, True)
    ambiguous = jnp.max(jnp.where(n_ge > kf, 1.0, 0.0)) > 0.0

    @pl.when(jnp.logical_not(ambiguous))
    def _():
        def body(kb, carry):
            sc_ref[:, blk(kb)] = jnp.where(sc_ref[:, blk(kb)] >= thr, 0.0, NEG_BIG)
            return carry
        lax.fori_loop(0, nvis, body, 0)

    @pl.when(ambiguous)
    def _():
        tri = tri_ref[...]

        def body(kb, carry):
            s = sc_ref[:, blk(kb)]
            eq = s == thr
            eqf = jnp.where(eq, 1.0, 0.0)
            before = carry + _dot(eqf.astype(BF16), tri)
            sel = (s > thr) | (eq & (before < need))
            sc_ref[:, blk(kb)] = jnp.where(sel, 0.0, NEG_BIG)
            return carry + jnp.sum(eqf, axis=1, keepdims=True)
        lax.fori_loop(0, nvis, body, jnp.zeros((tq, 1), F32))

    q = q_ref[...]
    low_k = lax.broadcasted_iota(I32, (tk, LANES), 1) < HEAD_DIM
    for g in range(N_HEADS // ATTN_GROUP):
        heads = range(g * ATTN_GROUP, (g + 1) * ATTN_GROUP)
        qm = {}
        for j in range(g * ATTN_GROUP // 2, (g + 1) * ATTN_GROUP // 2):
            qm[2 * j], qm[2 * j + 1] = head_masked(q[:, j * LANES:(j + 1) * LANES])
        m_ref[...] = jnp.full((ATTN_GROUP, tq, LANES), NEG_BIG, F32)
        acc_ref[...] = jnp.zeros((ATTN_GROUP, tq, LANES), F32)

        def body(kb, carry, heads=heads, qm=qm):
            bias = sc_ref[:, blk(kb)]

            def logits(h):
                return _dot(qm[h], kt_ref[(h // 2) * LANES:(h // 2 + 1) * LANES, blk(kb)])

            lg_next = logits(heads[0])
            for n, h in enumerate(heads):
                lg = lg_next + bias
                if n + 1 < len(heads):
                    lg_next = logits(heads[n + 1])
                vv = v_ref[blk(kb), (h // 2) * LANES:(h // 2 + 1) * LANES]
                vh = jnp.where(low_k if h % 2 == 0 else jnp.logical_not(low_k), vv, one_b)
                m_old = m_ref[n]
                m_new = jnp.maximum(m_old, jnp.max(lg, axis=1, keepdims=True))
                p = jnp.concatenate(
                    [jnp.exp(lg[:, c * LANES:(c + 1) * LANES] - m_new) for c in range(n_sub)], axis=1)
                acc_ref[n] = jnp.exp(m_old - m_new) * acc_ref[n] + _dot(p.astype(BF16), vh)
                m_ref[n] = m_new
            return carry

        lax.fori_loop(0, nvis, body, 0)
        for n in range(0, ATTN_GROUP, 2):
            outs = []
            for e in range(2):
                acc = acc_ref[n + e]
                outs.append(acc / pltpu.roll(acc, HEAD_DIM, 1))
            j = (g * ATTN_GROUP + n) // 2
            o_ref[:, j * LANES:(j + 1) * LANES] = jnp.where(low, outs[0], outs[1]).astype(BF16)


def _dsa(q, iq, iw, kt, v, ikt, tri, *, tq, tk, topk):
    B, S, d_attn = q.shape
    once = dict(pipeline_mode=pl.Buffered(1))
    return pl.pallas_call(
        functools.partial(_dsa_kernel, tq=tq, tk=tk, topk=topk),
        grid=(B, S // tq),
        in_specs=[
            pl.BlockSpec((None, tq, d_attn), lambda b, i: (b, i, 0)),
            pl.BlockSpec((None, tq, 2 * LANES), lambda b, i: (b, i, 0)),
            pl.BlockSpec((None, tq, LANES), lambda b, i: (b, i, 0)),
            pl.BlockSpec((d_attn, S), lambda b, i: (0, b), **once),
            pl.BlockSpec((None, S, d_attn), lambda b, i: (b, 0, 0), **once),
            pl.BlockSpec((LANES, S), lambda b, i: (0, b), **once),
            pl.BlockSpec((tk, tk), lambda b, i: (0, 0), **once),
        ],
        out_specs=pl.BlockSpec((None, tq, d_attn), lambda b, i: (b, i, 0)),
        out_shape=jax.ShapeDtypeStruct((B, S, d_attn), BF16),
        scratch_shapes=[pltpu.VMEM((tq, S), F32), pltpu.VMEM((ATTN_GROUP, tq, LANES), F32), pltpu.VMEM((ATTN_GROUP, tq, LANES), F32)],
        compiler_params=_cparams("arbitrary", "arbitrary"),
        name="dsa",
    )(q, iq, iw, kt, v, ikt, tri)


def _mix_kernel(x_ref, u_ref, uh_ref, sg_ref, ya_ref, pw_ref, pb_ref, ps_ref, wa_ref, wp_ref, wo_ref,
                g2_ref, rw_ref, rb_ref, x1_ref, h2_ref, lg_ref, *, ts, tiles_per_seq, d_model):
    it = pl.program_id(0) % tiles_per_seq
    u = u_ref[...]
    halo = jnp.where(it == 0, 0.0, uh_ref[...])
    ext = jnp.concatenate([halo, u], axis=0)
    t_in_seq = it * ts + lax.broadcasted_iota(I32, (ts, 1), 0)
    ys = []
    for g, w in enumerate(POOL_WINDOWS):
        s = ext[:, g * LANES:(g + 1) * LANES]
        d = 1
        while d < w:
            s = s + pltpu.roll(s, d, 0)
            d *= 2
        cnt = jnp.minimum(t_in_seq + 1, w).astype(F32)
        y = s[POOL_HALO:, :] / cnt - u[:, g * LANES:(g + 1) * LANES]
        ys.append(_dot(y.astype(BF16), pw_ref[g]))
    y_pool = (jnp.concatenate(ys, axis=1) + pb_ref[...]) * ps_ref[...]
    sg = sg_ref[...]
    merged = (sg[:, :d_model] * _dot(ya_ref[...], wa_ref[...])
              + sg[:, d_model:] * _dot(y_pool.astype(BF16), wp_ref[...]))
    x1 = x_ref[...] + _dot(merged.astype(BF16), wo_ref[...])
    x1_ref[...] = x1
    h2 = x1 * lax.rsqrt(jnp.mean(x1 * x1, axis=-1, keepdims=True) + EPS) * g2_ref[...]
    h2b = h2.astype(BF16)
    h2_ref[...] = h2b
    lg_ref[...] = _dot(h2b, rw_ref[...]) + rb_ref[...]


def _mix(x2, u, sg, ya, pw, pb, ps, wa, wp, wo, g2, rw, rb, *, ts, seq):
    T, d_model = x2.shape
    d_pool = u.shape[1]
    d_attn = ya.shape[1]
    tiles_per_seq = seq // ts
    halo_blocks = ts // POOL_HALO
    const2 = lambda i: (0, 0)
    row = lambda i: (i, 0)
    outs = [
        jax.ShapeDtypeStruct((T, d_model), F32),
        jax.ShapeDtypeStruct((T, d_model), BF16),
        jax.ShapeDtypeStruct((T, LANES), F32),
    ]
    return pl.pallas_call(
        functools.partial(_mix_kernel, ts=ts, tiles_per_seq=tiles_per_seq, d_model=d_model),
        grid=(T // ts,),
        in_specs=[
            pl.BlockSpec((ts, d_model), row),
            pl.BlockSpec((ts, d_pool), row),
            pl.BlockSpec((POOL_HALO, d_pool), lambda i: (jnp.maximum(i * halo_blocks - 1, 0), 0)),
            pl.BlockSpec((ts, 2 * d_model), row),
            pl.BlockSpec((ts, d_attn), row),
            pl.BlockSpec(pw.shape, lambda i: (0, 0, 0)),
            pl.BlockSpec((1, d_pool), const2),
            pl.BlockSpec((1, d_pool), const2),
            pl.BlockSpec(wa.shape, const2),
            pl.BlockSpec(wp.shape, const2),
            pl.BlockSpec(wo.shape, const2),
            pl.BlockSpec((1, d_model), const2),
            pl.BlockSpec(rw.shape, const2),
            pl.BlockSpec((1, LANES), const2),
        ],
        out_specs=[pl.BlockSpec((ts, s.shape[1]), row) for s in outs],
        out_shape=outs,
        compiler_params=_cparams("arbitrary"),
        name="mix",
    )(x2, u, u, sg, ya, pw, pb, ps, wa, wp, wo, g2, rw, rb)


def _router_kernel(lg_ref, lower_ref, upper_ref, lp_ref, w_ref, info_ref, carry_ref, *, ts):
    @pl.when(pl.program_id(0) == 0)
    def _():
        carry_ref[...] = jnp.zeros_like(carry_ref)

    l = lg_ref[...]
    lane = lax.broadcasted_iota(I32, (ts, LANES), 1)
    vals, hots = [], []
    for _ in range(TOP_K_EXPERTS):
        m = jnp.max(l, axis=1, keepdims=True)
        idx = jnp.min(jnp.where(l == m, lane, LANES), axis=1, keepdims=True)
        hot = lane == idx
        vals.append(m)
        hots.append(hot)
        l = jnp.where(hot, -jnp.inf, l)
    es = [jnp.exp(v - vals[0]) for v in vals]
    den = es[0] + es[1] + es[2] + es[3]
    picked = jnp.zeros((ts, LANES), F32)
    for hot in hots:
        picked = picked + jnp.where(hot, 1.0, 0.0)
    before = _dot(lower_ref[...], picked.astype(BF16))
    n = jnp.sum(picked, axis=0, keepdims=True)
    n_pad = jnp.floor((n + (ROW_ALIGN - 1)) * (1.0 / ROW_ALIGN)) * ROW_ALIGN
    run_start = _dot(jnp.broadcast_to(n_pad, (8, LANES)).astype(BF16), upper_ref[...])[:1]
    pos = before + run_start
    lp_o = jnp.zeros((ts, LANES), F32)
    w_o = jnp.zeros((ts, LANES), F32)
    for k in range(TOP_K_EXPERTS):
        slot = lane == k
        lp_o = jnp.where(slot, jnp.sum(jnp.where(hots[k], pos, 0.0), axis=1, keepdims=True), lp_o)
        w_o = jnp.where(slot, es[k] / den, w_o)
    lp_ref[...] = lp_o.astype(I32)
    w_ref[...] = w_o
    row = lax.broadcasted_iota(I32, (8, LANES), 0)
    info_ref[...] = jnp.where(row == 0, n_pad, jnp.where(row == 1, run_start, jnp.where(row == 2, carry_ref[...], 0.0)))
    carry_ref[...] = carry_ref[...] + n_pad


def _router(logits, lower, upper, *, ts):
    T = logits.shape[0]
    row = lambda i: (i, 0)
    const = lambda i: (0, 0)
    outs = [
        jax.ShapeDtypeStruct((T, LANES), I32),
        jax.ShapeDtypeStruct((T, LANES), F32),
        jax.ShapeDtypeStruct((T // ts * 8, LANES), F32),
    ]
    return pl.pallas_call(
        functools.partial(_router_kernel, ts=ts),
        grid=(T // ts,),
        in_specs=[pl.BlockSpec((ts, LANES), row), pl.BlockSpec((ts, ts), const), pl.BlockSpec((LANES, LANES), const)],
        out_specs=[pl.BlockSpec((ts, LANES), row), pl.BlockSpec((ts, LANES), row), pl.BlockSpec((8, LANES), row)],
        out_shape=outs,
        scratch_shapes=[pltpu.VMEM((1, LANES), F32)],
        compiler_params=_cparams("arbitrary"),
        name="router",
    )(logits, lower, upper)


def _run_chunks(largest):
    c, out = largest, []
    while c >= ROW_ALIGN:
        out.append(c)
        c //= 2
    return tuple(out)


def _for_each_piece(n, chunks, fn):
    off = jnp.int32(0)
    for c in chunks:
        @pl.when((n & c) != 0)
        def _(off=off, c=c):
            fn(off, c)
        off = off + (n & c)


def _for_each_run_chunk(i, n_ref, ls_ref, go_ref, chunks, fn):
    for e in range(N_EXPERTS):
        ls = ls_ref[i * N_EXPERTS + e]
        go = go_ref[i * N_EXPERTS + e]
        _for_each_piece(
            n_ref[i * N_EXPERTS + e], chunks,
            lambda off, c, ls=ls, go=go: fn(pl.multiple_of(ls + off, ROW_ALIGN), pl.multiple_of(go + off, ROW_ALIGN), c))


def _dispatch_kernel(n_ref, ls_ref, go_ref, tail_ref, tail_len_ref, lpt_ref, h_ref, xs_ref, xl_ref, zero_ref, sems,
                     *, ts, lp_rows):
    i = pl.program_id(0)
    slot = i % 2
    chunks = _run_chunks(ts)
    tail_chunks = _run_chunks(zero_ref.shape[0])

    @pl.when(i == 0)
    def _():
        zero_ref[...] = jnp.zeros_like(zero_ref)
        for e in range(N_EXPERTS):
            for wait in (False, True):
                def piece(off, c, e=e, wait=wait):
                    cp = pltpu.make_async_copy(
                        zero_ref.at[pl.ds(0, c)],
                        xs_ref.at[pl.ds(pl.multiple_of(tail_ref[e] + off, ROW_ALIGN), c)], sems.at[2])
                    cp.wait() if wait else cp.start()
                _for_each_piece(tail_len_ref[e], tail_chunks, piece)

        def trailing(k):
            row = pl.multiple_of(tail_ref[N_EXPERTS] + k * zero_ref.shape[0], zero_ref.shape[0])
            return pltpu.make_async_copy(zero_ref, xs_ref.at[pl.ds(row, zero_ref.shape[0])], sems.at[2])

        def start(k, c):
            trailing(k).start()
            return c

        def wait(k, c):
            trailing(k).wait()
            return c

        lax.fori_loop(0, tail_len_ref[N_EXPERTS], start, 0)
        lax.fori_loop(0, tail_len_ref[N_EXPERTS], wait, 0)

    h = h_ref[...]
    xl = xl_ref.at[slot]
    for r in range(lp_rows // PERM_ROWS):
        prow = r * PERM_ROWS + lax.broadcasted_iota(I32, (PERM_ROWS, ts), 0)
        hit = prow == lpt_ref[0:1, :]
        for k in range(1, TOP_K_EXPERTS):
            hit = hit | (prow == lpt_ref[k:k + 1, :])
        xl[r * PERM_ROWS:(r + 1) * PERM_ROWS, :] = _dot(jnp.where(hit, 1.0, 0.0).astype(BF16), h)

    def copy(buf, local_row, global_row, rows):
        return pltpu.make_async_copy(
            xl_ref.at[buf, pl.ds(local_row, rows)], xs_ref.at[pl.ds(global_row, rows)], sems.at[buf])

    _for_each_run_chunk(i, n_ref, ls_ref, go_ref, chunks, lambda a, b, c: copy(slot, a, b, c).start())

    @pl.when(i > 0)
    def _():
        _for_each_run_chunk(i - 1, n_ref, ls_ref, go_ref, chunks, lambda a, b, c: copy(1 - slot, a, b, c).wait())

    @pl.when(i == pl.num_programs(0) - 1)
    def _():
        _for_each_run_chunk(i, n_ref, ls_ref, go_ref, chunks, lambda a, b, c: copy(slot, a, b, c).wait())


def _dispatch(n_tab, ls_tab, go_tab, tail_tab, tail_len_tab, lpt, h2, *, ts, lp_rows, rows_out, tm):
    T, d = h2.shape
    grid_spec = pltpu.PrefetchScalarGridSpec(
        num_scalar_prefetch=5,
        grid=(T // ts,),
        in_specs=[
            pl.BlockSpec((8, ts), lambda i, *_: (0, i)),
            pl.BlockSpec((ts, d), lambda i, *_: (i, 0)),
        ],
        out_specs=pl.BlockSpec(memory_space=pl.ANY),
        scratch_shapes=[pltpu.VMEM((2, lp_rows, d), F32), pltpu.VMEM((tm // 2, d), F32),
                        pltpu.SemaphoreType.DMA((3,))],
    )
    return pl.pallas_call(
        functools.partial(_dispatch_kernel, ts=ts, lp_rows=lp_rows),
        grid_spec=grid_spec,
        out_shape=jax.ShapeDtypeStruct((rows_out, d), F32),
        compiler_params=_cparams("arbitrary"),
        name="dispatch",
    )(n_tab, ls_tab, go_tab, tail_tab, tail_len_tab, lpt, h2)


def _expert_kernel(te_ref, tv_ref, nv_ref, x_ref, w1_ref, b1_ref, w2_ref, b2_ref, y_ref, w1b_ref, w2b_ref, *, d_expert):
    del nv_ref
    i = pl.program_id(0)

    @pl.when(jnp.logical_or(i == 0, te_ref[i] != te_ref[jnp.maximum(i - 1, 0)]))
    def _():
        w1b_ref[...] = w1_ref[...].astype(BF16)
        w2b_ref[...] = w2_ref[...].astype(BF16)

    @pl.when(tv_ref[i] == 1)
    def _():
        gu = _dot(x_ref[...].astype(BF16), w1b_ref[...]) + b1_ref[...]
        gate = jnp.minimum(gu[:, :d_expert], SWIGLU_LIMIT)
        lin = jnp.clip(gu[:, d_expert:], -SWIGLU_LIMIT, SWIGLU_LIMIT)
        act = (lin + 1.0) * gate * jax.nn.sigmoid(SWIGLU_ALPHA * gate)
        y_ref[...] = _dot(act.astype(BF16), w2b_ref[...]) + b2_ref[...]

    @pl.when(tv_ref[i] == 0)
    def _():
        y_ref[...] = jnp.zeros_like(y_ref)


def _experts(tile_expert, tile_valid, n_valid, xs, w1, b1, w2, b2, *, tm):
    P, d = xs.shape
    d_expert = w2.shape[1]
    grid_spec = pltpu.PrefetchScalarGridSpec(
        num_scalar_prefetch=3,
        grid=(P // tm,),
        in_specs=[
            pl.BlockSpec((tm, d), lambda i, te, tv, nv: (jnp.minimum(i, nv[0] - 1), 0)),
            pl.BlockSpec((None, d, 2 * d_expert), lambda i, te, tv, nv: (te[i], 0, 0)),
            pl.BlockSpec((None, 1, 2 * d_expert), lambda i, te, tv, nv: (te[i], 0, 0)),
            pl.BlockSpec((None, d_expert, d), lambda i, te, tv, nv: (te[i], 0, 0)),
            pl.BlockSpec((None, 1, d), lambda i, te, tv, nv: (te[i], 0, 0)),
        ],
        out_specs=pl.BlockSpec((tm, d), lambda i, te, tv, nv: (i, 0)),
        scratch_shapes=[pltpu.VMEM((d, 2 * d_expert), BF16), pltpu.VMEM((d_expert, d), BF16)],
    )
    return pl.pallas_call(
        functools.partial(_expert_kernel, d_expert=d_expert),
        grid_spec=grid_spec,
        out_shape=jax.ShapeDtypeStruct((P, d), F32),
        compiler_params=_cparams("arbitrary"),
        name="experts",
    )(tile_expert, tile_valid, n_valid, xs, w1, b1, w2, b2)


def _combine_kernel(n_ref, ls_ref, go_ref, lp_ref, w_ref, x1_ref, y_ref, o_ref, yl_ref, sems, *, ts, lp_rows):
    i = pl.program_id(0)
    slot = i % 2
    chunks = _run_chunks(ts)

    def copy(buf, local_row, global_row, rows):
        return pltpu.make_async_copy(
            y_ref.at[pl.ds(global_row, rows)], yl_ref.at[buf, pl.ds(local_row, rows)], sems.at[buf])

    @pl.when(i == 0)
    def _():
        yl_ref[...] = jnp.zeros_like(yl_ref)
        _for_each_run_chunk(i, n_ref, ls_ref, go_ref, chunks, lambda a, b, c: copy(slot, a, b, c).start())

    @pl.when(i < pl.num_programs(0) - 1)
    def _():
        _for_each_run_chunk(i + 1, n_ref, ls_ref, go_ref, chunks, lambda a, b, c: copy(1 - slot, a, b, c).start())

    _for_each_run_chunk(i, n_ref, ls_ref, go_ref, chunks, lambda a, b, c: copy(slot, a, b, c).wait())

    lp = lp_ref[...]
    w = w_ref[...]
    lp_b = [jnp.broadcast_to(lp[:, k:k + 1], (ts, PERM_ROWS)) for k in range(TOP_K_EXPERTS)]
    w_b = [jnp.broadcast_to(w[:, k:k + 1], (ts, PERM_ROWS)) for k in range(TOP_K_EXPERTS)]
    yl = yl_ref.at[slot]
    out = x1_ref[...]
    for r in range(lp_rows // PERM_ROWS):
        pcol = r * PERM_ROWS + lax.broadcasted_iota(I32, (ts, PERM_ROWS), 1)
        wm = jnp.zeros((ts, PERM_ROWS), F32)
        for k in range(TOP_K_EXPERTS):
            wm = wm + jnp.where(pcol == lp_b[k], w_b[k], 0.0)
        out = out + _dot(wm.astype(BF16), yl[r * PERM_ROWS:(r + 1) * PERM_ROWS, :].astype(BF16))
    o_ref[...] = out


def _combine(n_tab, ls_tab, go_tab, lp, wts, x1, y, *, ts, lp_rows):
    T, d = x1.shape
    grid_spec = pltpu.PrefetchScalarGridSpec(
        num_scalar_prefetch=3,
        grid=(T // ts,),
        in_specs=[
            pl.BlockSpec((ts, LANES), lambda i, *_: (i, 0)),
            pl.BlockSpec((ts, LANES), lambda i, *_: (i, 0)),
            pl.BlockSpec((ts, d), lambda i, *_: (i, 0)),
            pl.BlockSpec(memory_space=pl.ANY),
        ],
        out_specs=pl.BlockSpec((ts, d), lambda i, *_: (i, 0)),
        scratch_shapes=[pltpu.VMEM((2, lp_rows, d), F32), pltpu.SemaphoreType.DMA((2,))],
    )
    return pl.pallas_call(
        functools.partial(_combine_kernel, ts=ts, lp_rows=lp_rows),
        grid_spec=grid_spec,
        out_shape=jax.ShapeDtypeStruct((T, d), F32),
        compiler_params=_cparams("arbitrary"),
        name="combine",
    )(n_tab, ls_tab, go_tab, lp, wts, x1, y)


def _pick(n, prefs):
    for p in prefs:
        if n % p == 0:
            return p
    raise ValueError(f"no tile in {prefs} divides {n}")


def _layer(x, attn_norm_g, w_in, q_norm_g, k_norm_g, pool_w, pool_b, pool_scale, w_branch_attn,
           w_branch_pool, w_out, ffn_norm_g, router_w, router_b, w1, b1, w2, b2):
    B, S, d_model = x.shape
    T = B * S
    d_attn = N_HEADS * HEAD_DIM
    d_pool = len(POOL_WINDOWS) * LANES
    d_idx = N_IDX_HEADS * IDX_DIM
    assert d_attn == d_model // 2 and d_pool == d_model // 2 and d_idx == 2 * LANES
    assert w_in.shape[1] == 3 * d_attn + d_idx + IDX_DIM + N_IDX_HEADS + d_pool + 2 * d_model

    o = 3 * d_attn
    w_iq = w_in[:, o:o + d_idx]
    w_ik = w_in[:, o + d_idx:o + d_idx + IDX_DIM]
    w_iw = w_in[:, o + d_idx + IDX_DIM:o + d_idx + IDX_DIM + N_IDX_HEADS]
    o2 = o + d_idx + IDX_DIM + N_IDX_HEADS
    w_cat = jnp.concatenate([
        w_in[:, :o], w_iq,
        jnp.pad(w_ik, ((0, 0), (0, LANES - IDX_DIM))),
        jnp.pad(w_iw, ((0, 0), (0, LANES - N_IDX_HEADS))),
        w_in[:, o2:]], axis=1).astype(BF16)
    head_of = jnp.arange(d_attn) // HEAD_DIM
    bd = (head_of[:, None] == head_of[None, :]).astype(BF16)

    x2 = x.reshape(T, d_model)
    q, kt, v, iq, ikt2, iw, u, sg = _inproj(
        x2, attn_norm_g.reshape(1, d_model), w_cat,
        jnp.tile(q_norm_g, N_HEADS).reshape(1, d_attn), jnp.tile(k_norm_g, N_HEADS).reshape(1, d_attn), bd,
        tm=_pick(T, (256, 128)), d_attn=d_attn, d_pool=d_pool, d_model=d_model)

    topk = min(TOPK_MAX, S // 4)
    tq = _pick(S, (512, 256, 128))
    tk = _pick(S, (512, 256, 128))
    tri = (jnp.arange(tk)[:, None] < jnp.arange(tk)[None, :]).astype(BF16)
    y_attn = _dsa(q.reshape(B, S, d_attn), iq.reshape(B, S, 2 * LANES), iw.reshape(B, S, LANES),
                  kt, v.reshape(B, S, d_attn), ikt2, tri, tq=tq, tk=tk, topk=topk)

    rw = jnp.pad(router_w, ((0, 0), (0, LANES - N_EXPERTS))).astype(BF16)
    rb = jnp.concatenate([router_b.astype(F32), jnp.full((LANES - N_EXPERTS,), NEG_BIG, F32)]).reshape(1, LANES)
    x1, h2, logits = _mix(
        x2, u, sg, y_attn.reshape(T, d_attn), pool_w.astype(BF16), pool_b.reshape(1, d_pool),
        pool_scale.reshape(1, d_pool), w_branch_attn.astype(BF16), w_branch_pool.astype(BF16),
        w_out.astype(BF16), ffn_norm_g.reshape(1, d_model), rw, rb, ts=_pick(S, (256, 128)), seq=S)

    tr = _pick(T, (512, 256, 128))
    n_tok_tiles = T // tr
    lp_rows = tr * TOP_K_EXPERTS + N_EXPERTS * ROW_ALIGN
    assert lp_rows % PERM_ROWS == 0
    lower = (jnp.arange(tr)[:, None] > jnp.arange(tr)[None, :]).astype(BF16)
    upper = (jnp.arange(LANES)[:, None] < jnp.arange(LANES)[None, :]).astype(BF16)
    lp, wts, info = _router(logits, lower, upper, ts=tr)
    info = info.reshape(n_tok_tiles, 8, LANES)[:, :, :N_EXPERTS].astype(I32)
    run_len, run_start, rows_before = info[:, 0], info[:, 1], info[:, 2]
    tm = EXPERT_ROWS
    total = rows_before[-1] + run_len[-1]
    group = (total + tm - 1) // tm * tm
    ends = jnp.cumsum(group)
    run_global = (ends - group)[None, :] + rows_before
    max_rows = T * TOP_K_EXPERTS + n_tok_tiles * N_EXPERTS * (ROW_ALIGN - 1) + N_EXPERTS * (tm - 1)
    n_tiles = (max_rows + tm - 1) // tm
    tile_row = jnp.arange(n_tiles, dtype=I32) * tm
    tile_valid = (tile_row < ends[-1]).astype(I32)
    tile_expert = jnp.minimum(jnp.sum((tile_row[:, None] >= ends[None, :]).astype(I32), axis=1), N_EXPERTS - 1)
    n_valid = (ends[-1:] // tm).astype(I32)
    tabs = (run_len.reshape(-1), run_start.reshape(-1), run_global.reshape(-1))

    lpt = jnp.pad(lp[:, :TOP_K_EXPERTS].T, ((0, 8 - TOP_K_EXPERTS), (0, 0)), constant_values=-1)
    rows_out = n_tiles * tm
    fill_start = jnp.concatenate([ends - group + total, ends[-1:]])
    fill_len = jnp.concatenate([group - total, (rows_out - ends[-1:]) // (tm // 2)])
    xs = _dispatch(*tabs, fill_start, fill_len, lpt, h2, ts=tr, lp_rows=lp_rows, rows_out=rows_out, tm=tm)
    y = _experts(tile_expert, tile_valid, n_valid, xs, w1, b1.reshape(N_EXPERTS, 1, -1),
                 w2, b2.reshape(N_EXPERTS, 1, -1), tm=tm)
    out = _combine(*tabs, lp, wts, x1, y, ts=tr, lp_rows=lp_rows)
    return out.reshape(B, S, d_model)


def kernel(x, attn_norm_g, w_in, q_norm_g, k_norm_g, pool_w, pool_b, pool_scale, w_branch_attn, w_branch_pool,
           w_out, ffn_norm_g, router_w, router_b, expert_w1, expert_b1, expert_w2, expert_b2):
    for l in range(attn_norm_g.shape[0]):
        x = _layer(x, attn_norm_g[l], w_in[l], q_norm_g[l], k_norm_g[l], pool_w[l], pool_b[l], pool_scale[l],
                   w_branch_attn[l], w_branch_pool[l], w_out[l], ffn_norm_g[l], router_w[l], router_b[l],
                   expert_w1[l], expert_b1[l], expert_w2[l], expert_b2[l])
    return x
```

```python
import functools

import jax
import jax.numpy as jnp
from jax import lax
from jax.experimental import pallas as pl
from jax.experimental.pallas import tpu as pltpu

F32 = jnp.float32
BF16 = jnp.bfloat16
I32 = jnp.int32

EPS = 1e-6
CHUNK = 64
HEAD_DIM = 64
N_HEADS = 8
N_IDX_HEADS = 4
IDX_DIM = 64
TOPK_MAX = 256
POOL_WINDOWS = (2, 4, 8, 16)
POOL_HALO = 16
N_EXPERTS = 32
TOP_K_EXPERTS = 4
SWIGLU_LIMIT = 7.0
SWIGLU_ALPHA = 1.702

LANES = 128
COUNT_ROWS = 64
COUNT_BLOCKS = 4
ATTN_GROUP = 8
ROW_ALIGN = 8
PERM_ROWS = 256
EXPERT_ROWS = 512
NEG_BIG = -1e30
INT_MIN = -(2 ** 31)
KEY_NEG_FLT_MAX = INT_MIN + 0x00800000
VMEM_LIMIT = 56 * 1024 * 1024


def _cparams(*sem):
    return pltpu.CompilerParams(dimension_semantics=sem, vmem_limit_bytes=VMEM_LIMIT)


def _dot(a, b):
    return jnp.dot(a, b, preferred_element_type=F32)


def _inproj_kernel(x_ref, g_ref, w_ref, gq_ref, gk_ref, bd_ref,
                   q_ref, kt_ref, v_ref, iq_ref, ikt_ref, iw_ref, u_ref, sg_ref, *, d_attn, d_pool):
    x = x_ref[...]
    h = x * lax.rsqrt(jnp.mean(x * x, axis=-1, keepdims=True) + EPS) * g_ref[...]
    hb = h.astype(BF16)
    bd = bd_ref[...]

    def proj(lo, hi):
        return _dot(hb, w_ref[:, lo:hi])

    def head_norm(t, g, scale):
        t2 = t * t
        hi = t2.astype(BF16)
        lo = (t2 - hi.astype(F32)).astype(BF16)
        ss = _dot(hi, bd) + _dot(lo, bd)
        return t * lax.rsqrt(ss * (1.0 / HEAD_DIM) + EPS) * (g * scale)

    o = 0
    q_ref[...] = head_norm(proj(o, o + d_attn), gq_ref[...], HEAD_DIM ** -0.5).astype(BF16)
    o += d_attn
    kt_ref[...] = head_norm(proj(o, o + d_attn), gk_ref[...], 1.0).T.astype(BF16)
    o += d_attn
    v_ref[...] = proj(o, o + d_attn).astype(BF16)
    o += d_attn
    idx = proj(o, o + 4 * LANES)
    iq_ref[...] = (idx[:, :2 * LANES] * (IDX_DIM ** -0.5)).astype(BF16)
    ikt = idx[:, 2 * LANES:3 * LANES].T
    ikt_ref[...] = (ikt + pltpu.roll(ikt, IDX_DIM, 0)).astype(BF16)
    iw_ref[...] = idx[:, 3 * LANES:] * (N_IDX_HEADS ** -0.5)
    o += 4 * LANES
    u_ref[...] = proj(o, o + d_pool)
    o += d_pool
    sg_ref[...] = jax.nn.sigmoid(proj(o, w_ref.shape[1]))


def _inproj(x2, g, w_cat, gq, gk, bd, *, tm, d_attn, d_pool, d_model):
    T = x2.shape[0]
    n = w_cat.shape[1]
    const = lambda i: (0, 0)
    row = lambda i: (i, 0)
    outs = [
        jax.ShapeDtypeStruct((T, d_attn), BF16),
        jax.ShapeDtypeStruct((d_attn, T), BF16),
        jax.ShapeDtypeStruct((T, d_attn), BF16),
        jax.ShapeDtypeStruct((T, 2 * LANES), BF16),
        jax.ShapeDtypeStruct((LANES, T), BF16),
        jax.ShapeDtypeStruct((T, LANES), F32),
        jax.ShapeDtypeStruct((T, d_pool), F32),
        jax.ShapeDtypeStruct((T, 2 * d_model), F32),
    ]
    return pl.pallas_call(
        functools.partial(_inproj_kernel, d_attn=d_attn, d_pool=d_pool),
        grid=(T // tm,),
        in_specs=[
            pl.BlockSpec((tm, d_model), row),
            pl.BlockSpec((1, d_model), const),
            pl.BlockSpec((d_model, n), const),
            pl.BlockSpec((1, d_attn), const),
            pl.BlockSpec((1, d_attn), const),
            pl.BlockSpec((d_attn, d_attn), const),
        ],
        out_specs=[pl.BlockSpec((s.shape[0], tm), lambda i: (0, i)) if n_out in (1, 4)
                   else pl.BlockSpec((tm, s.shape[1]), row) for n_out, s in enumerate(outs)],
        out_shape=outs,
        compiler_params=_cparams("arbitrary"),
        name="inproj",
    )(x2, g, w_cat, gq, gk, bd)


def _key_to_float(k):
    bits = jnp.where(k >= 0, k, k ^ 0x7FFFFFFF)
    return lax.bitcast_convert_type(bits, F32)


def _dsa_kernel(q_ref, iq_ref, iw_ref, kt_ref, v_ref, ikt_ref, tri_ref, o_ref, sc_ref, m_ref, acc_ref,
                *, tq, tk, topk):
    qi = pl.program_id(1)
    row0 = qi * tq
    nvis = (row0 + tq + tk - 1) // tk
    rows = row0 + lax.broadcasted_iota(I32, (tq, 1), 0)
    limit = (rows // CHUNK + 1) * CHUNK
    lane = lax.broadcasted_iota(I32, (tq, LANES), 1)
    low = lane < HEAD_DIM
    n_sub = tk // LANES
    zero_b = jnp.zeros((), BF16)
    one_b = jnp.ones((), BF16)

    def blk(kb):
        return pl.ds(pl.multiple_of(kb * tk, tk), tk)

    def head_masked(pair):
        return [jnp.where(low, pair, zero_b), jnp.where(low, zero_b, pair)]

    iq = iq_ref[...]
    iw = iw_ref[...]
    iq_h = []
    for j in range(N_IDX_HEADS // 2):
        iq_h += head_masked(iq[:, j * LANES:(j + 1) * LANES])
    iw_b = [jnp.broadcast_to(iw[:, h:h + 1], (tq, tk)) for h in range(N_IDX_HEADS)]
    col = lax.broadcasted_iota(I32, (tq, tk), 1)

    def score_body(kb, carry):
        ikt = ikt_ref[:, blk(kb)]
        s = jnp.zeros((tq, tk), F32)
        for h in range(N_IDX_HEADS):
            s = s + iw_b[h] * jnp.maximum(_dot(iq_h[h], ikt), 0.0)
        s = jnp.where(col + kb * tk < limit, s, -jnp.inf)
        sc_ref[:, blk(kb)] = s
        return carry

    lax.fori_loop(0, nvis, score_body, 0)

    n_rc = tq // COUNT_ROWS
    if tq == tk:
        n_full = nvis - 1
        diag_tiles = [((r + 1) * COUNT_ROWS + LANES - 1) // LANES for r in range(n_rc)]
    else:
        n_full, diag_tiles = nvis, None

    def count(cand, strict):
        cands = [jnp.broadcast_to(cand[r * COUNT_ROWS:(r + 1) * COUNT_ROWS], (COUNT_ROWS, LANES))
                 for r in range(n_rc)]

        def body_of(nblk):
            def body(kc, accs):
                out = []
                for r in range(n_rc):
                    acc = accs[r]
                    for j in range(nblk * n_sub):
                        s = sc_ref[r * COUNT_ROWS:(r + 1) * COUNT_ROWS,
                                   pl.ds(pl.multiple_of(kc * (nblk * tk) + j * LANES, LANES), LANES)]
                        hit = (s > cands[r]) if strict else (s >= cands[r])
                        acc = acc + jnp.where(hit, 1.0, 0.0)
                    out.append(acc)
                return tuple(out)
            return body

        accs = tuple(jnp.zeros((COUNT_ROWS, LANES), F32) for _ in range(n_rc))
        accs = lax.fori_loop(0, n_full // COUNT_BLOCKS, body_of(COUNT_BLOCKS), accs)
        accs = lax.fori_loop(n_full // COUNT_BLOCKS * COUNT_BLOCKS, n_full, body_of(1), accs)
        if diag_tiles is not None:
            accs = list(accs)
            for r in range(n_rc):
                for j in range(diag_tiles[r]):
                    s = sc_ref[r * COUNT_ROWS:(r + 1) * COUNT_ROWS,
                               pl.ds(pl.multiple_of(n_full * tk + j * LANES, LANES), LANES)]
                    hit = (s > cands[r]) if strict else (s >= cands[r])
                    accs[r] = accs[r] + jnp.where(hit, 1.0, 0.0)
        return jnp.concatenate([jnp.sum(a, axis=1, keepdims=True) for a in accs], axis=0)

    kf = float(topk)
    cnt0 = count(jnp.zeros((tq, 1), F32), False)
    ok0 = cnt0 >= kf
    c0 = jnp.where(ok0, 0, INT_MIN).astype(I32)
    n0 = jnp.where(ok0, cnt0, 0.0)

    def search_body(i, st):
        c, n_ge = st
        cand = c | jnp.left_shift(jnp.int32(1), 30 - i)
        cnt = count(_key_to_float(cand), False)
        ok = cnt >= kf
        return jnp.where(ok, cand, c), jnp.where(ok, cnt, n_ge)

    ckey, n_ge = lax.fori_loop(0, 31, search_body, (c0, n0))
    thr = _key_to_float(jnp.maximum(ckey, KEY_NEG_FLT_MAX))
    need = kf - count(thr, True)
    ambiguous = jnp.max(jnp.where(n_ge > kf, 1.0, 0.0)) > 0.0

    @pl.when(jnp.logical_not(ambiguous))
    def _():
        def body(kb, carry):
            sc_ref[:, blk(kb)] = jnp.where(sc_ref[:, blk(kb)] >= thr, 0.0, NEG_BIG)
            return carry
        lax.fori_loop(0, nvis, body, 0)

    @pl.when(ambiguous)
    def _():
        tri = tri_ref[...]

        def body(kb, carry):
            s = sc_ref[:, blk(kb)]
            eq = s == thr
            eqf = jnp.where(eq, 1.0, 0.0)
            before = carry + _dot(eqf.astype(BF16), tri)
            sel = (s > thr) | (eq & (before < need))
            sc_ref[:, blk(kb)] = jnp.where(sel, 0.0, NEG_BIG)
            return carry + jnp.sum(eqf, axis=1, keepdims=True)
        lax.fori_loop(0, nvis, body, jnp.zeros((tq, 1), F32))

    q = q_ref[...]
    low_k = lax.broadcasted_iota(I32, (tk, LANES), 1) < HEAD_DIM
    for g in range(N_HEADS // ATTN_GROUP):
        heads = range(g * ATTN_GROUP, (g + 1) * ATTN_GROUP)
        qm = {}
        for j in range(g * ATTN_GROUP // 2, (g + 1) * ATTN_GROUP // 2):
            qm[2 * j], qm[2 * j + 1] = head_masked(q[:, j * LANES:(j + 1) * LANES])
        m_ref[...] = jnp.full((ATTN_GROUP, tq, LANES), NEG_BIG, F32)
        acc_ref[...] = jnp.zeros((ATTN_GROUP, tq, LANES), F32)

        def body(kb, carry, heads=heads, qm=qm):
            bias = sc_ref[:, blk(kb)]

            def logits(h):
                return _dot(qm[h], kt_ref[(h // 2) * LANES:(h // 2 + 1) * LANES, blk(kb)])

            lg_next = logits(heads[0])
            for n, h in enumerate(heads):
                lg = lg_next + bias
                if n + 1 < len(heads):
                    lg_next = logits(heads[n + 1])
                vv = v_ref[blk(kb), (h // 2) * LANES:(h // 2 + 1) * LANES]
                vh = jnp.where(low_k if h % 2 == 0 else jnp.logical_not(low_k), vv, one_b)
                m_old = m_ref[n]
                m_new = jnp.maximum(m_old, jnp.max(lg, axis=1, keepdims=True))
                p = jnp.concatenate(
                    [jnp.exp(lg[:, c * LANES:(c + 1) * LANES] - m_new) for c in range(n_sub)], axis=1)
                acc_ref[n] = jnp.exp(m_old - m_new) * acc_ref[n] + _dot(p.astype(BF16), vh)
                m_ref[n] = m_new
            return carry

        lax.fori_loop(0, nvis, body, 0)
        for n in range(0, ATTN_GROUP, 2):
            outs = []
            for e in range(2):
                acc = acc_ref[n + e]
                outs.append(acc / pltpu.roll(acc, HEAD_DIM, 1))
            j = (g * ATTN_GROUP + n) // 2
            o_ref[:, j * LANES:(j + 1) * LANES] = jnp.where(low, outs[0], outs[1]).astype(BF16)


def _dsa(q, iq, iw, kt, v, ikt, tri, *, tq, tk, topk):
    B, S, d_attn = q.shape
    once = dict(pipeline_mode=pl.Buffered(1))
    return pl.pallas_call(
        functools.partial(_dsa_kernel, tq=tq, tk=tk, topk=topk),
        grid=(B, S // tq),
        in_specs=[
            pl.BlockSpec((None, tq, d_attn), lambda b, i: (b, i, 0)),
            pl.BlockSpec((None, tq, 2 * LANES), lambda b, i: (b, i, 0)),
            pl.BlockSpec((None, tq, LANES), lambda b, i: (b, i, 0)),
            pl.BlockSpec((d_attn, S), lambda b, i: (0, b), **once),
            pl.BlockSpec((None, S, d_attn), lambda b, i: (b, 0, 0), **once),
            pl.BlockSpec((LANES, S), lambda b, i: (0, b), **once),
            pl.BlockSpec((tk, tk), lambda b, i: (0, 0), **once),
        ],
        out_specs=pl.BlockSpec((None, tq, d_attn), lambda b, i: (b, i, 0)),
        out_shape=jax.ShapeDtypeStruct((B, S, d_attn), BF16),
        scratch_shapes=[pltpu.VMEM((tq, S), F32), pltpu.VMEM((ATTN_GROUP, tq, LANES), F32), pltpu.VMEM((ATTN_GROUP, tq, LANES), F32)],
        compiler_params=_cparams("arbitrary", "arbitrary"),
        name="dsa",
    )(q, iq, iw, kt, v, ikt, tri)


def _mix_kernel(x_ref, u_ref, uh_ref, sg_ref, ya_ref, pw_ref, pb_ref, ps_ref, wa_ref, wp_ref, wo_ref,
                g2_ref, rw_ref, rb_ref, x1_ref, h2_ref, lg_ref, *, ts, tiles_per_seq, d_model):
    it = pl.program_id(0) % tiles_per_seq
    u = u_ref[...]
    halo = jnp.where(it == 0, 0.0, uh_ref[...])
    ext = jnp.concatenate([halo, u], axis=0)
    t_in_seq = it * ts + lax.broadcasted_iota(I32, (ts, 1), 0)
    ys = []
    for g, w in enumerate(POOL_WINDOWS):
        s = ext[:, g * LANES:(g + 1) * LANES]
        d = 1
        while d < w:
            s = s + pltpu.roll(s, d, 0)
            d *= 2
        cnt = jnp.minimum(t_in_seq + 1, w).astype(F32)
        y = s[POOL_HALO:, :] / cnt - u[:, g * LANES:(g + 1) * LANES]
        ys.append(_dot(y.astype(BF16), pw_ref[g]))
    y_pool = (jnp.concatenate(ys, axis=1) + pb_ref[...]) * ps_ref[...]
    sg = sg_ref[...]
    merged = (sg[:, :d_model] * _dot(ya_ref[...], wa_ref[...])
              + sg[:, d_model:] * _dot(y_pool.astype(BF16), wp_ref[...]))
    x1 = x_ref[...] + _dot(merged.astype(BF16), wo_ref[...])
    x1_ref[...] = x1
    h2 = x1 * lax.rsqrt(jnp.mean(x1 * x1, axis=-1, keepdims=True) + EPS) * g2_ref[...]
    h2b = h2.astype(BF16)
    h2_ref[...] = h2b
    lg_ref[...] = _dot(h2b, rw_ref[...]) + rb_ref[...]


def _mix(x2, u, sg, ya, pw, pb, ps, wa, wp, wo, g2, rw, rb, *, ts, seq):
    T, d_model = x2.shape
    d_pool = u.shape[1]
    d_attn = ya.shape[1]
    tiles_per_seq = seq // ts
    halo_blocks = ts // POOL_HALO
    const2 = lambda i: (0, 0)
    row = lambda i: (i, 0)
    outs = [
        jax.ShapeDtypeStruct((T, d_model), F32),
        jax.ShapeDtypeStruct((T, d_model), BF16),
        jax.ShapeDtypeStruct((T, LANES), F32),
    ]
    return pl.pallas_call(
        functools.partial(_mix_kernel, ts=ts, tiles_per_seq=tiles_per_seq, d_model=d_model),
        grid=(T // ts,),
        in_specs=[
            pl.BlockSpec((ts, d_model), row),
            pl.BlockSpec((ts, d_pool), row),
            pl.BlockSpec((POOL_HALO, d_pool), lambda i: (jnp.maximum(i * halo_blocks - 1, 0), 0)),
            pl.BlockSpec((ts, 2 * d_model), row),
            pl.BlockSpec((ts, d_attn), row),
            pl.BlockSpec(pw.shape, lambda i: (0, 0, 0)),
            pl.BlockSpec((1, d_pool), const2),
            pl.BlockSpec((1, d_pool), const2),
            pl.BlockSpec(wa.shape, const2),
            pl.BlockSpec(wp.shape, const2),
            pl.BlockSpec(wo.shape, const2),
            pl.BlockSpec((1, d_model), const2),
            pl.BlockSpec(rw.shape, const2),
            pl.BlockSpec((1, LANES), const2),
        ],
        out_specs=[pl.BlockSpec((ts, s.shape[1]), row) for s in outs],
        out_shape=outs,
        compiler_params=_cparams("arbitrary"),
        name="mix",
    )(x2, u, u, sg, ya, pw, pb, ps, wa, wp, wo, g2, rw, rb)


def _router_kernel(lg_ref, lower_ref, upper_ref, lp_ref, w_ref, info_ref, carry_ref, *, ts):
    @pl.when(pl.program_id(0) == 0)
    def _():
        carry_ref[...] = jnp.zeros_like(carry_ref)

    l = lg_ref[...]
    lane = lax.broadcasted_iota(I32, (ts, LANES), 1)
    vals, hots = [], []
    for _ in range(TOP_K_EXPERTS):
        m = jnp.max(l, axis=1, keepdims=True)
        idx = jnp.min(jnp.where(l == m, lane, LANES), axis=1, keepdims=True)
        hot = lane == idx
        vals.append(m)
        hots.append(hot)
        l = jnp.where(hot, -jnp.inf, l)
    es = [jnp.exp(v - vals[0]) for v in vals]
    den = es[0] + es[1] + es[2] + es[3]
    picked = jnp.zeros((ts, LANES), F32)
    for hot in hots:
        picked = picked + jnp.where(hot, 1.0, 0.0)
    before = _dot(lower_ref[...], picked.astype(BF16))
    n = jnp.sum(picked, axis=0, keepdims=True)
    n_pad = jnp.floor((n + (ROW_ALIGN - 1)) * (1.0 / ROW_ALIGN)) * ROW_ALIGN
    run_start = _dot(jnp.broadcast_to(n_pad, (8, LANES)).astype(BF16), upper_ref[...])[:1]
    pos = before + run_start
    lp_o = jnp.zeros((ts, LANES), F32)
    w_o = jnp.zeros((ts, LANES), F32)
    for k in range(TOP_K_EXPERTS):
        slot = lane == k
        lp_o = jnp.where(slot, jnp.sum(jnp.where(hots[k], pos, 0.0), axis=1, keepdims=True), lp_o)
        w_o = jnp.where(slot, es[k] / den, w_o)
    lp_ref[...] = lp_o.astype(I32)
    w_ref[...] = w_o
    row = lax.broadcasted_iota(I32, (8, LANES), 0)
    info_ref[...] = jnp.where(row == 0, n_pad, jnp.where(row == 1, run_start, jnp.where(row == 2, carry_ref[...], 0.0)))
    carry_ref[...] = carry_ref[...] + n_pad


def _router(logits, lower, upper, *, ts):
    T = logits.shape[0]
    row = lambda i: (i, 0)
    const = lambda i: (0, 0)
    outs = [
        jax.ShapeDtypeStruct((T, LANES), I32),
        jax.ShapeDtypeStruct((T, LANES), F32),
        jax.ShapeDtypeStruct((T // ts * 8, LANES), F32),
    ]
    return pl.pallas_call(
        functools.partial(_router_kernel, ts=ts),
        grid=(T // ts,),
        in_specs=[pl.BlockSpec((ts, LANES), row), pl.BlockSpec((ts, ts), const), pl.BlockSpec((LANES, LANES), const)],
        out_specs=[pl.BlockSpec((ts, LANES), row), pl.BlockSpec((ts, LANES), row), pl.BlockSpec((8, LANES), row)],
        out_shape=outs,
        scratch_shapes=[pltpu.VMEM((1, LANES), F32)],
        compiler_params=_cparams("arbitrary"),
        name="router",
    )(logits, lower, upper)


def _run_chunks(largest):
    c, out = largest, []
    while c >= ROW_ALIGN:
        out.append(c)
        c //= 2
    return tuple(out)


def _for_each_piece(n, chunks, fn):
    off = jnp.int32(0)
    for c in chunks:
        @pl.when((n & c) != 0)
        def _(off=off, c=c):
            fn(off, c)
        off = off + (n & c)


def _for_each_run_chunk(i, n_ref, ls_ref, go_ref, chunks, fn):
    for e in range(N_EXPERTS):
        ls = ls_ref[i * N_EXPERTS + e]
        go = go_ref[i * N_EXPERTS + e]
        _for_each_piece(
            n_ref[i * N_EXPERTS + e], chunks,
            lambda off, c, ls=ls, go=go: fn(pl.multiple_of(ls + off, ROW_ALIGN), pl.multiple_of(go + off, ROW_ALIGN), c))


def _dispatch_kernel(n_ref, ls_ref, go_ref, tail_ref, tail_len_ref, lpt_ref, h_ref, xs_ref, xl_ref, zero_ref, sems,
                     *, ts, lp_rows):
    i = pl.program_id(0)
    slot = i % 2
    chunks = _run_chunks(ts)
    tail_chunks = _run_chunks(zero_ref.shape[0])

    @pl.when(i == 0)
    def _():
        zero_ref[...] = jnp.zeros_like(zero_ref)
        for e in range(N_EXPERTS):
            for wait in (False, True):
                def piece(off, c, e=e, wait=wait):
                    cp = pltpu.make_async_copy(
                        zero_ref.at[pl.ds(0, c)],
                        xs_ref.at[pl.ds(pl.multiple_of(tail_ref[e] + off, ROW_ALIGN), c)], sems.at[2])
                    cp.wait() if wait else cp.start()
                _for_each_piece(tail_len_ref[e], tail_chunks, piece)

        def trailing(k):
            row = pl.multiple_of(tail_ref[N_EXPERTS] + k * zero_ref.shape[0], zero_ref.shape[0])
            return pltpu.make_async_copy(zero_ref, xs_ref.at[pl.ds(row, zero_ref.shape[0])], sems.at[2])

        def start(k, c):
            trailing(k).start()
            return c

        def wait(k, c):
            trailing(k).wait()
            return c

        lax.fori_loop(0, tail_len_ref[N_EXPERTS], start, 0)
        lax.fori_loop(0, tail_len_ref[N_EXPERTS], wait, 0)

    h = h_ref[...]
    xl = xl_ref.at[slot]
    for r in range(lp_rows // PERM_ROWS):
        prow = r * PERM_ROWS + lax.broadcasted_iota(I32, (PERM_ROWS, ts), 0)
        hit = prow == lpt_ref[0:1, :]
        for k in range(1, TOP_K_EXPERTS):
            hit = hit | (prow == lpt_ref[k:k + 1, :])
        xl[r * PERM_ROWS:(r + 1) * PERM_ROWS, :] = _dot(jnp.where(hit, 1.0, 0.0).astype(BF16), h)

    def copy(buf, local_row, global_row, rows):
        return pltpu.make_async_copy(
            xl_ref.at[buf, pl.ds(local_row, rows)], xs_ref.at[pl.ds(global_row, rows)], sems.at[buf])

    _for_each_run_chunk(i, n_ref, ls_ref, go_ref, chunks, lambda a, b, c: copy(slot, a, b, c).start())

    @pl.when(i > 0)
    def _():
        _for_each_run_chunk(i - 1, n_ref, ls_ref, go_ref, chunks, lambda a, b, c: copy(1 - slot, a, b, c).wait())

    @pl.when(i == pl.num_programs(0) - 1)
    def _():
        _for_each_run_chunk(i, n_ref, ls_ref, go_ref, chunks, lambda a, b, c: copy(slot, a, b, c).wait())


def _dispatch(n_tab, ls_tab, go_tab, tail_tab, tail_len_tab, lpt, h2, *, ts, lp_rows, rows_out, tm):
    T, d = h2.shape
    grid_spec = pltpu.PrefetchScalarGridSpec(
        num_scalar_prefetch=5,
        grid=(T // ts,),
        in_specs=[
            pl.BlockSpec((8, ts), lambda i, *_: (0, i)),
            pl.BlockSpec((ts, d), lambda i, *_: (i, 0)),
        ],
        out_specs=pl.BlockSpec(memory_space=pl.ANY),
        scratch_shapes=[pltpu.VMEM((2, lp_rows, d), F32), pltpu.VMEM((tm // 2, d), F32),
                        pltpu.SemaphoreType.DMA((3,))],
    )
    return pl.pallas_call(
        functools.partial(_dispatch_kernel, ts=ts, lp_rows=lp_rows),
        grid_spec=grid_spec,
        out_shape=jax.ShapeDtypeStruct((rows_out, d), F32),
        compiler_params=_cparams("arbitrary"),
        name="dispatch",
    )(n_tab, ls_tab, go_tab, tail_tab, tail_len_tab, lpt, h2)


def _expert_kernel(te_ref, tv_ref, nv_ref, x_ref, w1_ref, b1_ref, w2_ref, b2_ref, y_ref, w1b_ref, w2b_ref, *, d_expert):
    del nv_ref
    i = pl.program_id(0)

    @pl.when(jnp.logical_or(i == 0, te_ref[i] != te_ref[jnp.maximum(i - 1, 0)]))
    def _():
        w1b_ref[...] = w1_ref[...].astype(BF16)
        w2b_ref[...] = w2_ref[...].astype(BF16)

    @pl.when(tv_ref[i] == 1)
    def _():
        gu = _dot(x_ref[...].astype(BF16), w1b_ref[...]) + b1_ref[...]
        gate = jnp.minimum(gu[:, :d_expert], SWIGLU_LIMIT)
        lin = jnp.clip(gu[:, d_expert:], -SWIGLU_LIMIT, SWIGLU_LIMIT)
        act = (lin + 1.0) * gate * jax.nn.sigmoid(SWIGLU_ALPHA * gate)
        y_ref[...] = _dot(act.astype(BF16), w2b_ref[...]) + b2_ref[...]

    @pl.when(tv_ref[i] == 0)
    def _():
        y_ref[...] = jnp.zeros_like(y_ref)


def _experts(tile_expert, tile_valid, n_valid, xs, w1, b1, w2, b2, *, tm):
    P, d = xs.shape
    d_expert = w2.shape[1]
    grid_spec = pltpu.PrefetchScalarGridSpec(
        num_scalar_prefetch=3,
        grid=(P // tm,),
        in_specs=[
            pl.BlockSpec((tm, d), lambda i, te, tv, nv: (jnp.minimum(i, nv[0] - 1), 0)),
            pl.BlockSpec((None, d, 2 * d_expert), lambda i, te, tv, nv: (te[i], 0, 0)),
            pl.BlockSpec((None, 1, 2 * d_expert), lambda i, te, tv, nv: (te[i], 0, 0)),
            pl.BlockSpec((None, d_expert, d), lambda i, te, tv, nv: (te[i], 0, 0)),
            pl.BlockSpec((None, 1, d), lambda i, te, tv, nv: (te[i], 0, 0)),
        ],
        out_specs=pl.BlockSpec((tm, d), lambda i, te, tv, nv: (i, 0)),
        scratch_shapes=[pltpu.VMEM((d, 2 * d_expert), BF16), pltpu.VMEM((d_expert, d), BF16)],
    )
    return pl.pallas_call(
        functools.partial(_expert_kernel, d_expert=d_expert),
        grid_spec=grid_spec,
        out_shape=jax.ShapeDtypeStruct((P, d), F32),
        compiler_params=_cparams("arbitrary"),
        name="experts",
    )(tile_expert, tile_valid, n_valid, xs, w1, b1, w2, b2)


def _combine_kernel(n_ref, ls_ref, go_ref, lp_ref, w_ref, x1_ref, y_ref, o_ref, yl_ref, sems, *, ts, lp_rows):
    i = pl.program_id(0)
    slot = i % 2
    chunks = _run_chunks(ts)

    def copy(buf, local_row, global_row, rows):
        return pltpu.make_async_copy(
            y_ref.at[pl.ds(global_row, rows)], yl_ref.at[buf, pl.ds(local_row, rows)], sems.at[buf])

    @pl.when(i == 0)
    def _():
        yl_ref[...] = jnp.zeros_like(yl_ref)
        _for_each_run_chunk(i, n_ref, ls_ref, go_ref, chunks, lambda a, b, c: copy(slot, a, b, c).start())

    @pl.when(i < pl.num_programs(0) - 1)
    def _():
        _for_each_run_chunk(i + 1, n_ref, ls_ref, go_ref, chunks, lambda a, b, c: copy(1 - slot, a, b, c).start())

    _for_each_run_chunk(i, n_ref, ls_ref, go_ref, chunks, lambda a, b, c: copy(slot, a, b, c).wait())

    lp = lp_ref[...]
    w = w_ref[...]
    lp_b = [jnp.broadcast_to(lp[:, k:k + 1], (ts, PERM_ROWS)) for k in range(TOP_K_EXPERTS)]
    w_b = [jnp.broadcast_to(w[:, k:k + 1], (ts, PERM_ROWS)) for k in range(TOP_K_EXPERTS)]
    yl = yl_ref.at[slot]
    out = x1_ref[...]
    for r in range(lp_rows // PERM_ROWS):
        pcol = r * PERM_ROWS + lax.broadcasted_iota(I32, (ts, PERM_ROWS), 1)
        wm = jnp.zeros((ts, PERM_ROWS), F32)
        for k in range(TOP_K_EXPERTS):
            wm = wm + jnp.where(pcol == lp_b[k], w_b[k], 0.0)
        out = out + _dot(wm.astype(BF16), yl[r * PERM_ROWS:(r + 1) * PERM_ROWS, :].astype(BF16))
    o_ref[...] = out


def _combine(n_tab, ls_tab, go_tab, lp, wts, x1, y, *, ts, lp_rows):
    T, d = x1.shape
    grid_spec = pltpu.PrefetchScalarGridSpec(
        num_scalar_prefetch=3,
        grid=(T // ts,),
        in_specs=[
            pl.BlockSpec((ts, LANES), lambda i, *_: (i, 0)),
            pl.BlockSpec((ts, LANES), lambda i, *_: (i, 0)),
            pl.BlockSpec((ts, d), lambda i, *_: (i, 0)),
            pl.BlockSpec(memory_space=pl.ANY),
        ],
        out_specs=pl.BlockSpec((ts, d), lambda i, *_: (i, 0)),
        scratch_shapes=[pltpu.VMEM((2, lp_rows, d), F32), pltpu.SemaphoreType.DMA((2,))],
    )
    return pl.pallas_call(
        functools.partial(_combine_kernel, ts=ts, lp_rows=lp_rows),
        grid_spec=grid_spec,
        out_shape=jax.ShapeDtypeStruct((T, d), F32),
        compiler_params=_cparams("arbitrary"),
        name="combine",
    )(n_tab, ls_tab, go_tab, lp, wts, x1, y)


def _pick(n, prefs):
    for p in prefs:
        if n % p == 0:
            return p
    raise ValueError(f"no tile in {prefs} divides {n}")


def _layer(x, attn_norm_g, w_in, q_norm_g, k_norm_g, pool_w, pool_b, pool_scale, w_branch_attn,
           w_branch_pool, w_out, ffn_norm_g, router_w, router_b, w1, b1, w2, b2):
    B, S, d_model = x.shape
    T = B * S
    d_attn = N_HEADS * HEAD_DIM
    d_pool = len(POOL_WINDOWS) * LANES
    d_idx = N_IDX_HEADS * IDX_DIM
    assert d_attn == d_model // 2 and d_pool == d_model // 2 and d_idx == 2 * LANES
    assert w_in.shape[1] == 3 * d_attn + d_idx + IDX_DIM + N_IDX_HEADS + d_pool + 2 * d_model

    o = 3 * d_attn
    w_iq = w_in[:, o:o + d_idx]
    w_ik = w_in[:, o + d_idx:o + d_idx + IDX_DIM]
    w_iw = w_in[:, o + d_idx + IDX_DIM:o + d_idx + IDX_DIM + N_IDX_HEADS]
    o2 = o + d_idx + IDX_DIM + N_IDX_HEADS
    w_cat = jnp.concatenate([
        w_in[:, :o], w_iq,
        jnp.pad(w_ik, ((0, 0), (0, LANES - IDX_DIM))),
        jnp.pad(w_iw, ((0, 0), (0, LANES - N_IDX_HEADS))),
        w_in[:, o2:]], axis=1).astype(BF16)
    head_of = jnp.arange(d_attn) // HEAD_DIM
    bd = (head_of[:, None] == head_of[None, :]).astype(BF16)

    x2 = x.reshape(T, d_model)
    q, kt, v, iq, ikt2, iw, u, sg = _inproj(
        x2, attn_norm_g.reshape(1, d_model), w_cat,
        jnp.tile(q_norm_g, N_HEADS).reshape(1, d_attn), jnp.tile(k_norm_g, N_HEADS).reshape(1, d_attn), bd,
        tm=_pick(T, (512, 256, 128)), d_attn=d_attn, d_pool=d_pool, d_model=d_model)

    topk = min(TOPK_MAX, S // 4)
    tq = _pick(S, (512, 256, 128))
    tk = _pick(S, (512, 256, 128))
    tri = (jnp.arange(tk)[:, None] < jnp.arange(tk)[None, :]).astype(BF16)
    y_attn = _dsa(q.reshape(B, S, d_attn), iq.reshape(B, S, 2 * LANES), iw.reshape(B, S, LANES),
                  kt, v.reshape(B, S, d_attn), ikt2, tri, tq=tq, tk=tk, topk=topk)

    rw = jnp.pad(router_w, ((0, 0), (0, LANES - N_EXPERTS))).astype(BF16)
    rb = jnp.concatenate([router_b.astype(F32), jnp.full((LANES - N_EXPERTS,), NEG_BIG, F32)]).reshape(1, LANES)
    x1, h2, logits = _mix(
        x2, u, sg, y_attn.reshape(T, d_attn), pool_w.astype(BF16), pool_b.reshape(1, d_pool),
        pool_scale.reshape(1, d_pool), w_branch_attn.astype(BF16), w_branch_pool.astype(BF16),
        w_out.astype(BF16), ffn_norm_g.reshape(1, d_model), rw, rb, ts=_pick(S, (256, 128)), seq=S)

    tr = _pick(T, (512, 256, 128))
    n_tok_tiles = T // tr
    lp_rows = tr * TOP_K_EXPERTS + N_EXPERTS * ROW_ALIGN
    assert lp_rows % PERM_ROWS == 0
    lower = (jnp.arange(tr)[:, None] > jnp.arange(tr)[None, :]).astype(BF16)
    upper = (jnp.arange(LANES)[:, None] < jnp.arange(LANES)[None, :]).astype(BF16)
    lp, wts, info = _router(logits, lower, upper, ts=tr)
    info = info.reshape(n_tok_tiles, 8, LANES)[:, :, :N_EXPERTS].astype(I32)
    run_len, run_start, rows_before = info[:, 0], info[:, 1], info[:, 2]
    tm = EXPERT_ROWS
    total = rows_before[-1] + run_len[-1]
    group = (total + tm - 1) // tm * tm
    ends = jnp.cumsum(group)
    run_global = (ends - group)[None, :] + rows_before
    max_rows = T * TOP_K_EXPERTS + n_tok_tiles * N_EXPERTS * (ROW_ALIGN - 1) + N_EXPERTS * (tm - 1)
    n_tiles = (max_rows + tm - 1) // tm
    tile_row = jnp.arange(n_tiles, dtype=I32) * tm
    tile_valid = (tile_row < ends[-1]).astype(I32)
    tile_expert = jnp.minimum(jnp.sum((tile_row[:, None] >= ends[None, :]).astype(I32), axis=1), N_EXPERTS - 1)
    n_valid = (ends[-1:] // tm).astype(I32)
    tabs = (run_len.reshape(-1), run_start.reshape(-1), run_global.reshape(-1))

    lpt = jnp.pad(lp[:, :TOP_K_EXPERTS].T, ((0, 8 - TOP_K_EXPERTS), (0, 0)), constant_values=-1)
    rows_out = n_tiles * tm
    fill_start = jnp.concatenate([ends - group + total, ends[-1:]])
    fill_len = jnp.concatenate([group - total, (rows_out - ends[-1:]) // (tm // 2)])
    xs = _dispatch(*tabs, fill_start, fill_len, lpt, h2, ts=tr, lp_rows=lp_rows, rows_out=rows_out, tm=tm)
    y = _experts(tile_expert, tile_valid, n_valid, xs, w1, b1.reshape(N_EXPERTS, 1, -1),
                 w2, b2.reshape(N_EXPERTS, 1, -1), tm=tm)
    out = _combine(*tabs, lp, wts, x1, y, ts=tr, lp_rows=lp_rows)
    return out.reshape(B, S, d_model)


def kernel(x, attn_norm_g, w_in, q_norm_g, k_norm_g, pool_w, pool_b, pool_scale, w_branch_attn, w_branch_pool,
           w_out, ffn_norm_g, router_w, router_b, expert_w1, expert_b1, expert_w2, expert_b2):
    for l in range(attn_norm_g.shape[0]):
        x = _layer(x, attn_norm_g[l], w_in[l], q_norm_g[l], k_norm_g[l], pool_w[l], pool_b[l], pool_scale[l],
                   w_branch_attn[l], w_branch_pool[l], w_out[l], ffn_norm_g[l], router_w[l], router_b[l],
                   expert_w1[l], expert_b1[l], expert_w2[l], expert_b2[l])
    return x
```

```python
import functools

import jax
import jax.numpy as jnp
from jax import lax
from jax.experimental import pallas as pl
from jax.experimental.pallas import tpu as pltpu

F32 = jnp.float32
BF16 = jnp.bfloat16
I32 = jnp.int32

EPS = 1e-6
CHUNK = 64
HEAD_DIM = 64
N_HEADS = 8
N_IDX_HEADS = 4
IDX_DIM = 64
TOPK_MAX = 256
POOL_WINDOWS = (2, 4, 8, 16)
POOL_HALO = 16
N_EXPERTS = 32
TOP_K_EXPERTS = 4
SWIGLU_LIMIT = 7.0
SWIGLU_ALPHA = 1.702

LANES = 128
COUNT_ROWS = 64
COUNT_BLOCKS = 4
TIE_ROWS = 128
ATTN_GROUP = 8
ROW_ALIGN = 8
LARGE_PIECE = 64
PERM_ROWS = 256
EXPERT_ROWS = 512
NEG_BIG = -1e30
INT_MIN = -(2 ** 31)
KEY_NEG_FLT_MAX = INT_MIN + 0x00800000
VMEM_LIMIT = 56 * 1024 * 1024


def _cparams(*sem):
    return pltpu.CompilerParams(dimension_semantics=sem, vmem_limit_bytes=VMEM_LIMIT)


def _dot(a, b):
    return jnp.dot(a, b, preferred_element_type=F32)


def _inproj_kernel(x_ref, g_ref, w_ref, gq_ref, gk_ref, bd_ref,
                   q_ref, kt_ref, v_ref, iq_ref, ikt_ref, iw_ref, u_ref, sg_ref, *, d_attn, d_pool):
    x = x_ref[...]
    h = x * lax.rsqrt(jnp.mean(x * x, axis=-1, keepdims=True) + EPS) * g_ref[...]
    hb = h.astype(BF16)
    bd = bd_ref[...]

    def proj(lo, hi):
        return _dot(hb, w_ref[:, lo:hi])

    def head_norm(t, g, scale):
        t2 = t * t
        hi = t2.astype(BF16)
        lo = (t2 - hi.astype(F32)).astype(BF16)
        ss = _dot(hi, bd) + _dot(lo, bd)
        return t * lax.rsqrt(ss * (1.0 / HEAD_DIM) + EPS) * (g * scale)

    o = 0
    q_ref[...] = head_norm(proj(o, o + d_attn), gq_ref[...], HEAD_DIM ** -0.5).astype(BF16)
    o += d_attn
    kt_ref[...] = head_norm(proj(o, o + d_attn), gk_ref[...], 1.0).T.astype(BF16)
    o += d_attn
    v_ref[...] = proj(o, o + d_attn).astype(BF16)
    o += d_attn
    idx = proj(o, o + 4 * LANES)
    iq_ref[...] = (idx[:, :2 * LANES] * (IDX_DIM ** -0.5)).astype(BF16)
    ikt = idx[:, 2 * LANES:3 * LANES].T
    ikt_ref[...] = (ikt + pltpu.roll(ikt, IDX_DIM, 0)).astype(BF16)
    iw_ref[...] = idx[:, 3 * LANES:] * (N_IDX_HEADS ** -0.5)
    o += 4 * LANES
    u_ref[...] = proj(o, o + d_pool)
    o += d_pool
    sg_ref[...] = jax.nn.sigmoid(proj(o, w_ref.shape[1]))


def _inproj(x2, g, w_cat, gq, gk, bd, *, tm, d_attn, d_pool, d_model):
    T = x2.shape[0]
    n = w_cat.shape[1]
    const = lambda i: (0, 0)
    row = lambda i: (i, 0)
    outs = [
        jax.ShapeDtypeStruct((T, d_attn), BF16),
        jax.ShapeDtypeStruct((d_attn, T), BF16),
        jax.ShapeDtypeStruct((T, d_attn), BF16),
        jax.ShapeDtypeStruct((T, 2 * LANES), BF16),
        jax.ShapeDtypeStruct((LANES, T), BF16),
        jax.ShapeDtypeStruct((T, LANES), F32),
        jax.ShapeDtypeStruct((T, d_pool), F32),
        jax.ShapeDtypeStruct((T, 2 * d_model), F32),
    ]
    return pl.pallas_call(
        functools.partial(_inproj_kernel, d_attn=d_attn, d_pool=d_pool),
        grid=(T // tm,),
        in_specs=[
            pl.BlockSpec((tm, d_model), row),
            pl.BlockSpec((1, d_model), const),
            pl.BlockSpec((d_model, n), const),
            pl.BlockSpec((1, d_attn), const),
            pl.BlockSpec((1, d_attn), const),
            pl.BlockSpec((d_attn, d_attn), const),
        ],
        out_specs=[pl.BlockSpec((s.shape[0], tm), lambda i: (0, i)) if n_out in (1, 4)
                   else pl.BlockSpec((tm, s.shape[1]), row) for n_out, s in enumerate(outs)],
        out_shape=outs,
        compiler_params=_cparams("arbitrary"),
        name="inproj",
    )(x2, g, w_cat, gq, gk, bd)


def _key_to_float(k):
    bits = jnp.where(k >= 0, k, k ^ 0x7FFFFFFF)
    return lax.bitcast_convert_type(bits, F32)


def _dsa_kernel(q_ref, iq_ref, iw_ref, kt_ref, v_ref, ikt_ref, tri_ref, o_ref, sc_ref, m_ref, acc_ref,
                *, tq, tk, topk):
    qi = pl.program_id(1)
    row0 = qi * tq
    nvis = (row0 + tq + tk - 1) // tk
    rows = row0 + lax.broadcasted_iota(I32, (tq, 1), 0)
    limit = (rows // CHUNK + 1) * CHUNK
    lane = lax.broadcasted_iota(I32, (tq, LANES), 1)
    low = lane < HEAD_DIM
    n_sub = tk // LANES
    zero_b = jnp.zeros((), BF16)
    one_b = jnp.ones((), BF16)

    def blk(kb):
        return pl.ds(pl.multiple_of(kb * tk, tk), tk)

    def head_masked(pair):
        return [jnp.where(low, pair, zero_b), jnp.where(low, zero_b, pair)]

    iq = iq_ref[...]
    iw = iw_ref[...]
    iq_h = []
    for j in range(N_IDX_HEADS // 2):
        iq_h += head_masked(iq[:, j * LANES:(j + 1) * LANES])
    iw_b = [jnp.broadcast_to(iw[:, h:h + 1], (tq, tk)) for h in range(N_IDX_HEADS)]
    col = lax.broadcasted_iota(I32, (tq, tk), 1)

    def score_body(kb, carry):
        ikt = ikt_ref[:, blk(kb)]
        s = jnp.zeros((tq, tk), F32)
        for h in range(N_IDX_HEADS):
            s = s + iw_b[h] * jnp.maximum(_dot(iq_h[h], ikt), 0.0)
        s = jnp.where(col + kb * tk < limit, s, -jnp.inf)
        sc_ref[:, blk(kb)] = s
        return carry

    lax.fori_loop(0, nvis, score_body, 0)

    n_rc = tq // COUNT_ROWS
    if tq == tk:
        n_full = nvis - 1
        diag_tiles = [((r + 1) * COUNT_ROWS + LANES - 1) // LANES for r in range(n_rc)]
    else:
        n_full, diag_tiles = nvis, None

    def count(cand, strict):
        cands = [jnp.broadcast_to(cand[r * COUNT_ROWS:(r + 1) * COUNT_ROWS], (COUNT_ROWS, LANES))
                 for r in range(n_rc)]

        def body_of(nblk):
            def body(kc, accs):
                out = []
                for r in range(n_rc):
                    acc = accs[r]
                    for j in range(nblk * n_sub):
                        s = sc_ref[r * COUNT_ROWS:(r + 1) * COUNT_ROWS,
                                   pl.ds(pl.multiple_of(kc * (nblk * tk) + j * LANES, LANES), LANES)]
                        hit = (s > cands[r]) if strict else (s >= cands[r])
                        acc = acc + jnp.where(hit, 1.0, 0.0)
                    out.append(acc)
                return tuple(out)
            return body

        accs = tuple(jnp.zeros((COUNT_ROWS, LANES), F32) for _ in range(n_rc))
        accs = lax.fori_loop(0, n_full // COUNT_BLOCKS, body_of(COUNT_BLOCKS), accs)
        accs = lax.fori_loop(n_full // COUNT_BLOCKS * COUNT_BLOCKS, n_full, body_of(1), accs)
        if diag_tiles is not None:
            accs = list(accs)
            for r in range(n_rc):
                for j in range(diag_tiles[r]):
                    s = sc_ref[r * COUNT_ROWS:(r + 1) * COUNT_ROWS,
                               pl.ds(pl.multiple_of(n_full * tk + j * LANES, LANES), LANES)]
                    hit = (s > cands[r]) if strict else (s >= cands[r])
                    accs[r] = accs[r] + jnp.where(hit, 1.0, 0.0)
        return jnp.concatenate([jnp.sum(a, axis=1, keepdims=True) for a in accs], axis=0)

    kf = float(topk)
    cnt0 = count(jnp.zeros((tq, 1), F32), False)
    ok0 = cnt0 >= kf
    c0 = jnp.where(ok0, 0, INT_MIN).astype(I32)
    n0 = jnp.where(ok0, cnt0, 0.0)

    def search_body(i, st):
        c, n_ge = st
        cand = c | jnp.left_shift(jnp.int32(1), 30 - i)
        cnt = count(_key_to_float(cand), False)
        ok = cnt >= kf
        return jnp.where(ok, cand, c), jnp.where(ok, cnt, n_ge)

    ckey, n_ge = lax.fori_loop(0, 31, search_body, (c0, n0))
    thr = _key_to_float(jnp.maximum(ckey, KEY_NEG_FLT_MAX))
    need = kf - count(thr, True)
    ambiguous = jnp.max(jnp.where(n_ge > kf, 1.0, 0.0)) > 0.0

    @pl.when(jnp.logical_not(ambiguous))
    def _():
        def body(kb, carry):
            sc_ref[:, blk(kb)] = jnp.where(sc_ref[:, blk(kb)] >= thr, 0.0, NEG_BIG)
            return carry
        lax.fori_loop(0, nvis, body, 0)

    @pl.when(ambiguous)
    def _():
        tri = tri_ref[...]

        def body(kb, carries):
            out = []
            for r in range(tq // TIE_ROWS):
                rows = slice(r * TIE_ROWS, (r + 1) * TIE_ROWS)
                s = sc_ref[rows, blk(kb)]
                eq = s == thr[rows]
                eqf = jnp.where(eq, 1.0, 0.0)
                before = carries[r] + _dot(eqf.astype(BF16), tri)
                sel = (s > thr[rows]) | (eq & (before < need[rows]))
                sc_ref[rows, blk(kb)] = jnp.where(sel, 0.0, NEG_BIG)
                out.append(before[:, tk - 1:] + eqf[:, tk - 1:])
            return tuple(out)
        lax.fori_loop(0, nvis, body, tuple(jnp.zeros((TIE_ROWS, 1), F32) for _ in range(tq // TIE_ROWS)))

    q = q_ref[...]
    low_k = lax.broadcasted_iota(I32, (tk, LANES), 1) < HEAD_DIM
    for g in range(N_HEADS // ATTN_GROUP):
        heads = range(g * ATTN_GROUP, (g + 1) * ATTN_GROUP)
        qm = {}
        for j in range(g * ATTN_GROUP // 2, (g + 1) * ATTN_GROUP // 2):
            qm[2 * j], qm[2 * j + 1] = head_masked(q[:, j * LANES:(j + 1) * LANES])
        m_ref[...] = jnp.full((ATTN_GROUP, tq, LANES), NEG_BIG, F32)
        acc_ref[...] = jnp.zeros((ATTN_GROUP, tq, LANES), F32)

        def body(kb, carry, heads=heads, qm=qm):
            bias = sc_ref[:, blk(kb)]

            def logits(h):
                return _dot(qm[h], kt_ref[(h // 2) * LANES:(h // 2 + 1) * LANES, blk(kb)])

            lg_next = logits(heads[0])
            for n, h in enumerate(heads):
                lg = lg_next + bias
                if n + 1 < len(heads):
                    lg_next = logits(heads[n + 1])
                vv = v_ref[blk(kb), (h // 2) * LANES:(h // 2 + 1) * LANES]
                vh = jnp.where(low_k if h % 2 == 0 else jnp.logical_not(low_k), vv, one_b)
                m_old = m_ref[n]
                m_new = jnp.maximum(m_old, jnp.max(lg, axis=1, keepdims=True))
                p = jnp.concatenate(
                    [jnp.exp(lg[:, c * LANES:(c + 1) * LANES] - m_new) for c in range(n_sub)], axis=1)
                acc_ref[n] = jnp.exp(m_old - m_new) * acc_ref[n] + _dot(p.astype(BF16), vh)
                m_ref[n] = m_new
            return carry

        lax.fori_loop(0, nvis, body, 0)
        for n in range(0, ATTN_GROUP, 2):
            outs = []
            for e in range(2):
                acc = acc_ref[n + e]
                outs.append(acc / pltpu.roll(acc, HEAD_DIM, 1))
            j = (g * ATTN_GROUP + n) // 2
            o_ref[:, j * LANES:(j + 1) * LANES] = jnp.where(low, outs[0], outs[1]).astype(BF16)


def _dsa(q, iq, iw, kt, v, ikt, tri, *, tq, tk, topk):
    B, S, d_attn = q.shape
    once = dict(pipeline_mode=pl.Buffered(1))
    return pl.pallas_call(
        functools.partial(_dsa_kernel, tq=tq, tk=tk, topk=topk),
        grid=(B, S // tq),
        in_specs=[
            pl.BlockSpec((None, tq, d_attn), lambda b, i: (b, i, 0)),
            pl.BlockSpec((None, tq, 2 * LANES), lambda b, i: (b, i, 0)),
            pl.BlockSpec((None, tq, LANES), lambda b, i: (b, i, 0)),
            pl.BlockSpec((d_attn, S), lambda b, i: (0, b), **once),
            pl.BlockSpec((None, S, d_attn), lambda b, i: (b, 0, 0), **once),
            pl.BlockSpec((LANES, S), lambda b, i: (0, b), **once),
            pl.BlockSpec((tk, tk), lambda b, i: (0, 0), **once),
        ],
        out_specs=pl.BlockSpec((None, tq, d_attn), lambda b, i: (b, i, 0)),
        out_shape=jax.ShapeDtypeStruct((B, S, d_attn), BF16),
        scratch_shapes=[pltpu.VMEM((tq, S), F32), pltpu.VMEM((ATTN_GROUP, tq, LANES), F32), pltpu.VMEM((ATTN_GROUP, tq, LANES), F32)],
        compiler_params=_cparams("arbitrary", "arbitrary"),
        name="dsa",
    )(q, iq, iw, kt, v, ikt, tri)


def _mix_kernel(x_ref, u_ref, uh_ref, sg_ref, ya_ref, pw_ref, pb_ref, ps_ref, wa_ref, wp_ref, wo_ref,
                g2_ref, rw_ref, rb_ref, x1_ref, h2_ref, lg_ref, *, ts, tiles_per_seq, d_model):
    it = pl.program_id(0) % tiles_per_seq
    u = u_ref[...]
    halo = jnp.where(it == 0, 0.0, uh_ref[...])
    ext = jnp.concatenate([halo, u], axis=0)
    t_in_seq = it * ts + lax.broadcasted_iota(I32, (ts, 1), 0)
    ys = []
    for g, w in enumerate(POOL_WINDOWS):
        s = ext[:, g * LANES:(g + 1) * LANES]
        d = 1
        while d < w:
            s = s + pltpu.roll(s, d, 0)
            d *= 2
        cnt = jnp.minimum(t_in_seq + 1, w).astype(F32)
        y = s[POOL_HALO:, :] / cnt - u[:, g * LANES:(g + 1) * LANES]
        ys.append(_dot(y.astype(BF16), pw_ref[g]))
    y_pool = (jnp.concatenate(ys, axis=1) + pb_ref[...]) * ps_ref[...]
    sg = sg_ref[...]
    merged = (sg[:, :d_model] * _dot(ya_ref[...], wa_ref[...])
              + sg[:, d_model:] * _dot(y_pool.astype(BF16), wp_ref[...]))
    x1 = x_ref[...] + _dot(merged.astype(BF16), wo_ref[...])
    x1_ref[...] = x1
    h2 = x1 * lax.rsqrt(jnp.mean(x1 * x1, axis=-1, keepdims=True) + EPS) * g2_ref[...]
    h2b = h2.astype(BF16)
    h2_ref[...] = h2b
    lg_ref[...] = _dot(h2b, rw_ref[...]) + rb_ref[...]


def _mix(x2, u, sg, ya, pw, pb, ps, wa, wp, wo, g2, rw, rb, *, ts, seq):
    T, d_model = x2.shape
    d_pool = u.shape[1]
    d_attn = ya.shape[1]
    tiles_per_seq = seq // ts
    halo_blocks = ts // POOL_HALO
    const2 = lambda i: (0, 0)
    row = lambda i: (i, 0)
    outs = [
        jax.ShapeDtypeStruct((T, d_model), F32),
        jax.ShapeDtypeStruct((T, d_model), BF16),
        jax.ShapeDtypeStruct((T, LANES), F32),
    ]
    return pl.pallas_call(
        functools.partial(_mix_kernel, ts=ts, tiles_per_seq=tiles_per_seq, d_model=d_model),
        grid=(T // ts,),
        in_specs=[
            pl.BlockSpec((ts, d_model), row),
            pl.BlockSpec((ts, d_pool), row),
            pl.BlockSpec((POOL_HALO, d_pool), lambda i: (jnp.maximum(i * halo_blocks - 1, 0), 0)),
            pl.BlockSpec((ts, 2 * d_model), row),
            pl.BlockSpec((ts, d_attn), row),
            pl.BlockSpec(pw.shape, lambda i: (0, 0, 0)),
            pl.BlockSpec((1, d_pool), const2),
            pl.BlockSpec((1, d_pool), const2),
            pl.BlockSpec(wa.shape, const2),
            pl.BlockSpec(wp.shape, const2),
            pl.BlockSpec(wo.shape, const2),
            pl.BlockSpec((1, d_model), const2),
            pl.BlockSpec(rw.shape, const2),
            pl.BlockSpec((1, LANES), const2),
        ],
        out_specs=[pl.BlockSpec((ts, s.shape[1]), row) for s in outs],
        out_shape=outs,
        compiler_params=_cparams("arbitrary"),
        name="mix",
    )(x2, u, u, sg, ya, pw, pb, ps, wa, wp, wo, g2, rw, rb)


def _router_kernel(lg_ref, lower_ref, upper_ref, lp_ref, w_ref, info_ref, carry_ref, *, ts):
    @pl.when(pl.program_id(0) == 0)
    def _():
        carry_ref[...] = jnp.zeros_like(carry_ref)

    l = lg_ref[...]
    lane = lax.broadcasted_iota(I32, (ts, LANES), 1)
    vals, hots = [], []
    for _ in range(TOP_K_EXPERTS):
        m = jnp.max(l, axis=1, keepdims=True)
        idx = jnp.min(jnp.where(l == m, lane, LANES), axis=1, keepdims=True)
        hot = lane == idx
        vals.append(m)
        hots.append(hot)
        l = jnp.where(hot, -jnp.inf, l)
    es = [jnp.exp(v - vals[0]) for v in vals]
    den = es[0] + es[1] + es[2] + es[3]
    picked = jnp.zeros((ts, LANES), F32)
    for hot in hots:
        picked = picked + jnp.where(hot, 1.0, 0.0)
    before = _dot(lower_ref[...], picked.astype(BF16))
    n = jnp.sum(picked, axis=0, keepdims=True)
    n_pad = jnp.floor((n + (ROW_ALIGN - 1)) * (1.0 / ROW_ALIGN)) * ROW_ALIGN
    run_start = _dot(jnp.broadcast_to(n_pad, (8, LANES)).astype(BF16), upper_ref[...])[:1]
    pos = before + run_start
    lp_o = jnp.zeros((ts, LANES), F32)
    w_o = jnp.zeros((ts, LANES), F32)
    for k in range(TOP_K_EXPERTS):
        slot = lane == k
        lp_o = jnp.where(slot, jnp.sum(jnp.where(hots[k], pos, 0.0), axis=1, keepdims=True), lp_o)
        w_o = jnp.where(slot, es[k] / den, w_o)
    lp_ref[...] = lp_o.astype(I32)
    w_ref[...] = w_o
    row = lax.broadcasted_iota(I32, (8, LANES), 0)
    info_ref[...] = jnp.where(row == 0, n_pad, jnp.where(row == 1, run_start, jnp.where(row == 2, carry_ref[...], 0.0)))
    carry_ref[...] = carry_ref[...] + n_pad


def _router(logits, lower, upper, *, ts):
    T = logits.shape[0]
    row = lambda i: (i, 0)
    const = lambda i: (0, 0)
    outs = [
        jax.ShapeDtypeStruct((T, LANES), I32),
        jax.ShapeDtypeStruct((T, LANES), F32),
        jax.ShapeDtypeStruct((T // ts * 8, LANES), F32),
    ]
    return pl.pallas_call(
        functools.partial(_router_kernel, ts=ts),
        grid=(T // ts,),
        in_specs=[pl.BlockSpec((ts, LANES), row), pl.BlockSpec((ts, ts), const), pl.BlockSpec((LANES, LANES), const)],
        out_specs=[pl.BlockSpec((ts, LANES), row), pl.BlockSpec((ts, LANES), row), pl.BlockSpec((8, LANES), row)],
        out_shape=outs,
        scratch_shapes=[pltpu.VMEM((1, LANES), F32)],
        compiler_params=_cparams("arbitrary"),
        name="router",
    )(logits, lower, upper)


def _run_chunks(largest):
    c, out = largest, []
    while c >= ROW_ALIGN:
        out.append(c)
        c //= 2
    return tuple(out)


def _for_each_piece(n, chunks, fn):
    def pieces(sizes, off):
        for c in sizes:
            @pl.when((n & c) != 0)
            def _(off=off, c=c):
                fn(off, c)
            off = off + (n & c)
        return off

    large = tuple(c for c in chunks if c >= LARGE_PIECE)
    small = tuple(c for c in chunks if c < LARGE_PIECE)
    if large:
        @pl.when(n >= LARGE_PIECE)
        def _():
            pieces(large, jnp.int32(0))
    pieces(small, n & ~jnp.int32(LARGE_PIECE - 1))


def _for_each_run_chunk(i, n_ref, ls_ref, go_ref, chunks, fn):
    for e in range(N_EXPERTS):
        ls = ls_ref[i * N_EXPERTS + e]
        go = go_ref[i * N_EXPERTS + e]
        _for_each_piece(
            n_ref[i * N_EXPERTS + e], chunks,
            lambda off, c, ls=ls, go=go: fn(pl.multiple_of(ls + off, ROW_ALIGN), pl.multiple_of(go + off, ROW_ALIGN), c))


def _dispatch_kernel(n_ref, ls_ref, go_ref, tail_ref, tail_len_ref, lpt_ref, h_ref, xs_ref, xl_ref, zero_ref, sems,
                     *, ts, lp_rows):
    i = pl.program_id(0)
    slot = i % 2
    chunks = _run_chunks(ts)
    tail_chunks = _run_chunks(zero_ref.shape[0])

    @pl.when(i == 0)
    def _():
        zero_ref[...] = jnp.zeros_like(zero_ref)
        for e in range(N_EXPERTS):
            for wait in (False, True):
                def piece(off, c, e=e, wait=wait):
                    cp = pltpu.make_async_copy(
                        zero_ref.at[pl.ds(0, c)],
                        xs_ref.at[pl.ds(pl.multiple_of(tail_ref[e] + off, ROW_ALIGN), c)], sems.at[2])
                    cp.wait() if wait else cp.start()
                _for_each_piece(tail_len_ref[e], tail_chunks, piece)

        def trailing(k):
            row = pl.multiple_of(tail_ref[N_EXPERTS] + k * zero_ref.shape[0], zero_ref.shape[0])
            return pltpu.make_async_copy(zero_ref, xs_ref.at[pl.ds(row, zero_ref.shape[0])], sems.at[2])

        def start(k, c):
            trailing(k).start()
            return c

        def wait(k, c):
            trailing(k).wait()
            return c

        lax.fori_loop(0, tail_len_ref[N_EXPERTS], start, 0)
        lax.fori_loop(0, tail_len_ref[N_EXPERTS], wait, 0)

    h = h_ref[...]
    xl = xl_ref.at[slot]
    for r in range(lp_rows // PERM_ROWS):
        prow = r * PERM_ROWS + lax.broadcasted_iota(I32, (PERM_ROWS, ts), 0)
        hit = prow == lpt_ref[0:1, :]
        for k in range(1, TOP_K_EXPERTS):
            hit = hit | (prow == lpt_ref[k:k + 1, :])
        xl[r * PERM_ROWS:(r + 1) * PERM_ROWS, :] = _dot(jnp.where(hit, 1.0, 0.0).astype(BF16), h)

    def copy(buf, local_row, global_row, rows):
        return pltpu.make_async_copy(
            xl_ref.at[buf, pl.ds(local_row, rows)], xs_ref.at[pl.ds(global_row, rows)], sems.at[buf])

    _for_each_run_chunk(i, n_ref, ls_ref, go_ref, chunks, lambda a, b, c: copy(slot, a, b, c).start())

    @pl.when(i > 0)
    def _():
        _for_each_run_chunk(i - 1, n_ref, ls_ref, go_ref, chunks, lambda a, b, c: copy(1 - slot, a, b, c).wait())

    @pl.when(i == pl.num_programs(0) - 1)
    def _():
        _for_each_run_chunk(i, n_ref, ls_ref, go_ref, chunks, lambda a, b, c: copy(slot, a, b, c).wait())


def _dispatch(n_tab, ls_tab, go_tab, tail_tab, tail_len_tab, lpt, h2, *, ts, lp_rows, rows_out, tm):
    T, d = h2.shape
    grid_spec = pltpu.PrefetchScalarGridSpec(
        num_scalar_prefetch=5,
        grid=(T // ts,),
        in_specs=[
            pl.BlockSpec((8, ts), lambda i, *_: (0, i)),
            pl.BlockSpec((ts, d), lambda i, *_: (i, 0)),
        ],
        out_specs=pl.BlockSpec(memory_space=pl.ANY),
        scratch_shapes=[pltpu.VMEM((2, lp_rows, d), F32), pltpu.VMEM((tm // 2, d), F32),
                        pltpu.SemaphoreType.DMA((3,))],
    )
    return pl.pallas_call(
        functools.partial(_dispatch_kernel, ts=ts, lp_rows=lp_rows),
        grid_spec=grid_spec,
        out_shape=jax.ShapeDtypeStruct((rows_out, d), F32),
        compiler_params=_cparams("arbitrary"),
        name="dispatch",
    )(n_tab, ls_tab, go_tab, tail_tab, tail_len_tab, lpt, h2)


def _expert_kernel(te_ref, tv_ref, nv_ref, x_ref, w1_ref, b1_ref, w2_ref, b2_ref, y_ref, w1b_ref, w2b_ref, *, d_expert):
    del nv_ref
    i = pl.program_id(0)

    @pl.when(jnp.logical_or(i == 0, te_ref[i] != te_ref[jnp.maximum(i - 1, 0)]))
    def _():
        w1b_ref[...] = w1_ref[...].astype(BF16)
        w2b_ref[...] = w2_ref[...].astype(BF16)

    @pl.when(tv_ref[i] == 1)
    def _():
        gu = _dot(x_ref[...].astype(BF16), w1b_ref[...]) + b1_ref[...]
        gate = jnp.minimum(gu[:, :d_expert], SWIGLU_LIMIT)
        lin = jnp.clip(gu[:, d_expert:], -SWIGLU_LIMIT, SWIGLU_LIMIT)
        act = (lin + 1.0) * gate * jax.nn.sigmoid(SWIGLU_ALPHA * gate)
        y_ref[...] = _dot(act.astype(BF16), w2b_ref[...]) + b2_ref[...]

    @pl.when(tv_ref[i] == 0)
    def _():
        y_ref[...] = jnp.zeros_like(y_ref)


def _experts(tile_expert, tile_valid, n_valid, xs, w1, b1, w2, b2, *, tm):
    P, d = xs.shape
    d_expert = w2.shape[1]
    grid_spec = pltpu.PrefetchScalarGridSpec(
        num_scalar_prefetch=3,
        grid=(P // tm,),
        in_specs=[
            pl.BlockSpec((tm, d), lambda i, te, tv, nv: (jnp.minimum(i, nv[0] - 1), 0)),
            pl.BlockSpec((None, d, 2 * d_expert), lambda i, te, tv, nv: (te[i], 0, 0)),
            pl.BlockSpec((None, 1, 2 * d_expert), lambda i, te, tv, nv: (te[i], 0, 0)),
            pl.BlockSpec((None, d_expert, d), lambda i, te, tv, nv: (te[i], 0, 0)),
            pl.BlockSpec((None, 1, d), lambda i, te, tv, nv: (te[i], 0, 0)),
        ],
        out_specs=pl.BlockSpec((tm, d), lambda i, te, tv, nv: (i, 0)),
        scratch_shapes=[pltpu.VMEM((d, 2 * d_expert), BF16), pltpu.VMEM((d_expert, d), BF16)],
    )
    return pl.pallas_call(
        functools.partial(_expert_kernel, d_expert=d_expert),
        grid_spec=grid_spec,
        out_shape=jax.ShapeDtypeStruct((P, d), F32),
        compiler_params=_cparams("arbitrary"),
        name="experts",
    )(tile_expert, tile_valid, n_valid, xs, w1, b1, w2, b2)


def _combine_kernel(n_ref, ls_ref, go_ref, lp_ref, w_ref, x1_ref, y_ref, o_ref, yl_ref, sems, *, ts, lp_rows):
    i = pl.program_id(0)
    slot = i % 2
    chunks = _run_chunks(ts)

    def copy(buf, local_row, global_row, rows):
        return pltpu.make_async_copy(
            y_ref.at[pl.ds(global_row, rows)], yl_ref.at[buf, pl.ds(local_row, rows)], sems.at[buf])

    @pl.when(i == 0)
    def _():
        yl_ref[...] = jnp.zeros_like(yl_ref)
        _for_each_run_chunk(i, n_ref, ls_ref, go_ref, chunks, lambda a, b, c: copy(slot, a, b, c).start())

    @pl.when(i < pl.num_programs(0) - 1)
    def _():
        _for_each_run_chunk(i + 1, n_ref, ls_ref, go_ref, chunks, lambda a, b, c: copy(1 - slot, a, b, c).start())

    _for_each_run_chunk(i, n_ref, ls_ref, go_ref, chunks, lambda a, b, c: copy(slot, a, b, c).wait())

    lp = lp_ref[...]
    w = w_ref[...]
    lp_b = [jnp.broadcast_to(lp[:, k:k + 1], (ts, PERM_ROWS)) for k in range(TOP_K_EXPERTS)]
    w_b = [jnp.broadcast_to(w[:, k:k + 1], (ts, PERM_ROWS)) for k in range(TOP_K_EXPERTS)]
    yl = yl_ref.at[slot]
    out = x1_ref[...]
    for r in range(lp_rows // PERM_ROWS):
        pcol = r * PERM_ROWS + lax.broadcasted_iota(I32, (ts, PERM_ROWS), 1)
        wm = jnp.zeros((ts, PERM_ROWS), F32)
        for k in range(TOP_K_EXPERTS):
            wm = wm + jnp.where(pcol == lp_b[k], w_b[k], 0.0)
        out = out + _dot(wm.astype(BF16), yl[r * PERM_ROWS:(r + 1) * PERM_ROWS, :].astype(BF16))
    o_ref[...] = out


def _combine(n_tab, ls_tab, go_tab, lp, wts, x1, y, *, ts, lp_rows):
    T, d = x1.shape
    grid_spec = pltpu.PrefetchScalarGridSpec(
        num_scalar_prefetch=3,
        grid=(T // ts,),
        in_specs=[
            pl.BlockSpec((ts, LANES), lambda i, *_: (i, 0)),
            pl.BlockSpec((ts, LANES), lambda i, *_: (i, 0)),
            pl.BlockSpec((ts, d), lambda i, *_: (i, 0)),
            pl.BlockSpec(memory_space=pl.ANY),
        ],
        out_specs=pl.BlockSpec((ts, d), lambda i, *_: (i, 0)),
        scratch_shapes=[pltpu.VMEM((2, lp_rows, d), F32), pltpu.SemaphoreType.DMA((2,))],
    )
    return pl.pallas_call(
        functools.partial(_combine_kernel, ts=ts, lp_rows=lp_rows),
        grid_spec=grid_spec,
        out_shape=jax.ShapeDtypeStruct((T, d), F32),
        compiler_params=_cparams("arbitrary"),
        name="combine",
    )(n_tab, ls_tab, go_tab, lp, wts, x1, y)


def _pick(n, prefs):
    for p in prefs:
        if n % p == 0:
            return p
    raise ValueError(f"no tile in {prefs} divides {n}")


def _layer(x, attn_norm_g, w_in, q_norm_g, k_norm_g, pool_w, pool_b, pool_scale, w_branch_attn,
           w_branch_pool, w_out, ffn_norm_g, router_w, router_b, w1, b1, w2, b2):
    B, S, d_model = x.shape
    T = B * S
    d_attn = N_HEADS * HEAD_DIM
    d_pool = len(POOL_WINDOWS) * LANES
    d_idx = N_IDX_HEADS * IDX_DIM
    assert d_attn == d_model // 2 and d_pool == d_model // 2 and d_idx == 2 * LANES
    assert w_in.shape[1] == 3 * d_attn + d_idx + IDX_DIM + N_IDX_HEADS + d_pool + 2 * d_model

    o = 3 * d_attn
    w_iq = w_in[:, o:o + d_idx]
    w_ik = w_in[:, o + d_idx:o + d_idx + IDX_DIM]
    w_iw = w_in[:, o + d_idx + IDX_DIM:o + d_idx + IDX_DIM + N_IDX_HEADS]
    o2 = o + d_idx + IDX_DIM + N_IDX_HEADS
    w_cat = jnp.concatenate([
        w_in[:, :o], w_iq,
        jnp.pad(w_ik, ((0, 0), (0, LANES - IDX_DIM))),
        jnp.pad(w_iw, ((0, 0), (0, LANES - N_IDX_HEADS))),
        w_in[:, o2:]], axis=1).astype(BF16)
    head_of = jnp.arange(d_attn) // HEAD_DIM
    bd = (head_of[:, None] == head_of[None, :]).astype(BF16)

    x2 = x.reshape(T, d_model)
    q, kt, v, iq, ikt2, iw, u, sg = _inproj(
        x2, attn_norm_g.reshape(1, d_model), w_cat,
        jnp.tile(q_norm_g, N_HEADS).reshape(1, d_attn), jnp.tile(k_norm_g, N_HEADS).reshape(1, d_attn), bd,
        tm=_pick(T, (512, 256, 128)), d_attn=d_attn, d_pool=d_pool, d_model=d_model)

    topk = min(TOPK_MAX, S // 4)
    tq = _pick(S, (512, 256, 128))
    tk = _pick(S, (512, 256, 128))
    tri = (jnp.arange(tk)[:, None] < jnp.arange(tk)[None, :]).astype(BF16)
    y_attn = _dsa(q.reshape(B, S, d_attn), iq.reshape(B, S, 2 * LANES), iw.reshape(B, S, LANES),
                  kt, v.reshape(B, S, d_attn), ikt2, tri, tq=tq, tk=tk, topk=topk)

    rw = jnp.pad(router_w, ((0, 0), (0, LANES - N_EXPERTS))).astype(BF16)
    rb = jnp.concatenate([router_b.astype(F32), jnp.full((LANES - N_EXPERTS,), NEG_BIG, F32)]).reshape(1, LANES)
    x1, h2, logits = _mix(
        x2, u, sg, y_attn.reshape(T, d_attn), pool_w.astype(BF16), pool_b.reshape(1, d_pool),
        pool_scale.reshape(1, d_pool), w_branch_attn.astype(BF16), w_branch_pool.astype(BF16),
        w_out.astype(BF16), ffn_norm_g.reshape(1, d_model), rw, rb, ts=_pick(S, (256, 128)), seq=S)

    tr = _pick(T, (512, 256, 128))
    n_tok_tiles = T // tr
    lp_rows = tr * TOP_K_EXPERTS + N_EXPERTS * ROW_ALIGN
    assert lp_rows % PERM_ROWS == 0
    lower = (jnp.arange(tr)[:, None] > jnp.arange(tr)[None, :]).astype(BF16)
    upper = (jnp.arange(LANES)[:, None] < jnp.arange(LANES)[None, :]).astype(BF16)
    lp, wts, info = _router(logits, lower, upper, ts=tr)
    info = info.reshape(n_tok_tiles, 8, LANES)[:, :, :N_EXPERTS].astype(I32)
    run_len, run_start, rows_before = info[:, 0], info[:, 1], info[:, 2]
    tm = EXPERT_ROWS
    total = rows_before[-1] + run_len[-1]
    group = (total + tm - 1) // tm * tm
    ends = jnp.cumsum(group)
    run_global = (ends - group)[None, :] + rows_before
    max_rows = T * TOP_K_EXPERTS + n_tok_tiles * N_EXPERTS * (ROW_ALIGN - 1) + N_EXPERTS * (tm - 1)
    n_tiles = (max_rows + tm - 1) // tm
    tile_row = jnp.arange(n_tiles, dtype=I32) * tm
    tile_valid = (tile_row < ends[-1]).astype(I32)
    tile_expert = jnp.minimum(jnp.sum((tile_row[:, None] >= ends[None, :]).astype(I32), axis=1), N_EXPERTS - 1)
    n_valid = (ends[-1:] // tm).astype(I32)
    tabs = (run_len.reshape(-1), run_start.reshape(-1), run_global.reshape(-1))

    lpt = jnp.pad(lp[:, :TOP_K_EXPERTS].T, ((0, 8 - TOP_K_EXPERTS), (0, 0)), constant_values=-1)
    rows_out = n_tiles * tm
    fill_start = jnp.concatenate([ends - group + total, ends[-1:]])
    fill_len = jnp.concatenate([group - total, (rows_out - ends[-1:]) // (tm // 2)])
    xs = _dispatch(*tabs, fill_start, fill_len, lpt, h2, ts=tr, lp_rows=lp_rows, rows_out=rows_out, tm=tm)
    y = _experts(tile_expert, tile_valid, n_valid, xs, w1, b1.reshape(N_EXPERTS, 1, -1),
                 w2, b2.reshape(N_EXPERTS, 1, -1), tm=tm)
    out = _combine(*tabs, lp, wts, x1, y, ts=tr, lp_rows=lp_rows)
    return out.reshape(B, S, d_model)


def kernel(x, attn_norm_g, w_in, q_norm_g, k_norm_g, pool_w, pool_b, pool_scale, w_branch_attn, w_branch_pool,
           w_out, ffn_norm_g, router_w, router_b, expert_w1, expert_b1, expert_w2, expert_b2):
    for l in range(attn_norm_g.shape[0]):
        x = _layer(x, attn_norm_g[l], w_in[l], q_norm_g[l], k_norm_g[l], pool_w[l], pool_b[l], pool_scale[l],
                   w_branch_attn[l], w_branch_pool[l], w_out[l], ffn_norm_g[l], router_w[l], router_b[l],
                   expert_w1[l], expert_b1[l], expert_w2[l], expert_b2[l])
    return x
```

```python
import functools

import jax
import jax.numpy as jnp
from jax import lax
from jax.experimental import pallas as pl
from jax.experimental.pallas import tpu as pltpu

F32 = jnp.float32
BF16 = jnp.bfloat16
I32 = jnp.int32

EPS = 1e-6
CHUNK = 64
HEAD_DIM = 64
N_HEADS = 8
N_IDX_HEADS = 4
IDX_DIM = 64
TOPK_MAX = 256
POOL_WINDOWS = (2, 4, 8, 16)
POOL_HALO = 16
N_EXPERTS = 32
TOP_K_EXPERTS = 4
SWIGLU_LIMIT = 7.0
SWIGLU_ALPHA = 1.702

LANES = 128
COUNT_ROWS = 64
COUNT_BLOCKS = 4
TIE_ROWS = 128
ATTN_GROUP = 8
ROW_ALIGN = 8
LARGE_PIECE = 64
PERM_ROWS = 256
EXPERT_ROWS = 512
NEG_BIG = -1e30
INT_MIN = -(2 ** 31)
KEY_NEG_FLT_MAX = INT_MIN + 0x00800000
VMEM_LIMIT = 56 * 1024 * 1024


def _cparams(*sem):
    return pltpu.CompilerParams(dimension_semantics=sem, vmem_limit_bytes=VMEM_LIMIT)


def _dot(a, b):
    return jnp.dot(a, b, preferred_element_type=F32)


def _inproj_kernel(x_ref, g_ref, w_ref, gq_ref, gk_ref, bd_ref,
                   q_ref, kt_ref, v_ref, iq_ref, ikt_ref, iw_ref, u_ref, *, d_attn, d_pool):
    x = x_ref[...]
    h = x * lax.rsqrt(jnp.mean(x * x, axis=-1, keepdims=True) + EPS) * g_ref[...]
    hb = h.astype(BF16)
    bd = bd_ref[...]

    def proj(lo, hi):
        return _dot(hb, w_ref[:, lo:hi])

    def head_norm(t, g, scale):
        t2 = t * t
        hi = t2.astype(BF16)
        lo = (t2 - hi.astype(F32)).astype(BF16)
        ss = _dot(hi, bd) + _dot(lo, bd)
        return t * lax.rsqrt(ss * (1.0 / HEAD_DIM) + EPS) * (g * scale)

    o = 0
    q_ref[...] = head_norm(proj(o, o + d_attn), gq_ref[...], HEAD_DIM ** -0.5).astype(BF16)
    o += d_attn
    kt_ref[...] = head_norm(proj(o, o + d_attn), gk_ref[...], 1.0).T.astype(BF16)
    o += d_attn
    v_ref[...] = proj(o, o + d_attn).astype(BF16)
    o += d_attn
    idx = proj(o, o + 4 * LANES)
    iq_ref[...] = (idx[:, :2 * LANES] * (IDX_DIM ** -0.5)).astype(BF16)
    ikt = idx[:, 2 * LANES:3 * LANES].T
    ikt_ref[...] = (ikt + pltpu.roll(ikt, IDX_DIM, 0)).astype(BF16)
    iw_ref[...] = idx[:, 3 * LANES:] * (N_IDX_HEADS ** -0.5)
    o += 4 * LANES
    u_ref[...] = proj(o, w_ref.shape[1])


def _inproj(x2, g, w_cat, gq, gk, bd, *, tm, d_attn, d_pool, d_model):
    T = x2.shape[0]
    n = w_cat.shape[1]
    const = lambda i: (0, 0)
    row = lambda i: (i, 0)
    outs = [
        jax.ShapeDtypeStruct((T, d_attn), BF16),
        jax.ShapeDtypeStruct((d_attn, T), BF16),
        jax.ShapeDtypeStruct((T, d_attn), BF16),
        jax.ShapeDtypeStruct((T, 2 * LANES), BF16),
        jax.ShapeDtypeStruct((LANES, T), BF16),
        jax.ShapeDtypeStruct((T, LANES), F32),
        jax.ShapeDtypeStruct((T, d_pool), F32),
    ]
    return pl.pallas_call(
        functools.partial(_inproj_kernel, d_attn=d_attn, d_pool=d_pool),
        grid=(T // tm,),
        in_specs=[
            pl.BlockSpec((tm, d_model), row),
            pl.BlockSpec((1, d_model), const),
            pl.BlockSpec((d_model, n), const),
            pl.BlockSpec((1, d_attn), const),
            pl.BlockSpec((1, d_attn), const),
            pl.BlockSpec((d_attn, d_attn), const),
        ],
        out_specs=[pl.BlockSpec((s.shape[0], tm), lambda i: (0, i)) if n_out in (1, 4)
                   else pl.BlockSpec((tm, s.shape[1]), row) for n_out, s in enumerate(outs)],
        out_shape=outs,
        compiler_params=_cparams("arbitrary"),
        name="inproj",
    )(x2, g, w_cat, gq, gk, bd)


def _key_to_float(k):
    bits = jnp.where(k >= 0, k, k ^ 0x7FFFFFFF)
    return lax.bitcast_convert_type(bits, F32)


def _dsa_kernel(q_ref, iq_ref, iw_ref, kt_ref, v_ref, ikt_ref, tri_ref, o_ref, sc_ref, m_ref, acc_ref,
                *, tq, tk, topk):
    qi = pl.program_id(1)
    row0 = qi * tq
    nvis = (row0 + tq + tk - 1) // tk
    rows = row0 + lax.broadcasted_iota(I32, (tq, 1), 0)
    limit = (rows // CHUNK + 1) * CHUNK
    lane = lax.broadcasted_iota(I32, (tq, LANES), 1)
    low = lane < HEAD_DIM
    n_sub = tk // LANES
    zero_b = jnp.zeros((), BF16)
    one_b = jnp.ones((), BF16)

    def blk(kb):
        return pl.ds(pl.multiple_of(kb * tk, tk), tk)

    def head_masked(pair):
        return [jnp.where(low, pair, zero_b), jnp.where(low, zero_b, pair)]

    iq = iq_ref[...]
    iw = iw_ref[...]
    iq_h = []
    for j in range(N_IDX_HEADS // 2):
        iq_h += head_masked(iq[:, j * LANES:(j + 1) * LANES])
    iw_b = [jnp.broadcast_to(iw[:, h:h + 1], (tq, tk)) for h in range(N_IDX_HEADS)]
    col = lax.broadcasted_iota(I32, (tq, tk), 1)

    def score_body(kb, carry):
        ikt = ikt_ref[:, blk(kb)]
        s = jnp.zeros((tq, tk), F32)
        for h in range(N_IDX_HEADS):
            s = s + iw_b[h] * jnp.maximum(_dot(iq_h[h], ikt), 0.0)
        s = jnp.where(col + kb * tk < limit, s, -jnp.inf)
        sc_ref[:, blk(kb)] = s
        return carry

    lax.fori_loop(0, nvis, score_body, 0)

    n_rc = tq // COUNT_ROWS
    if tq == tk:
        n_full = nvis - 1
        diag_tiles = [((r + 1) * COUNT_ROWS + LANES - 1) // LANES for r in range(n_rc)]
    else:
        n_full, diag_tiles = nvis, None

    def count(cand, strict):
        cands = [jnp.broadcast_to(cand[r * COUNT_ROWS:(r + 1) * COUNT_ROWS], (COUNT_ROWS, LANES))
                 for r in range(n_rc)]

        def body_of(nblk):
            def body(kc, accs):
                out = []
                for r in range(n_rc):
                    acc = accs[r]
                    for j in range(nblk * n_sub):
                        s = sc_ref[r * COUNT_ROWS:(r + 1) * COUNT_ROWS,
                                   pl.ds(pl.multiple_of(kc * (nblk * tk) + j * LANES, LANES), LANES)]
                        hit = (s > cands[r]) if strict else (s >= cands[r])
                        acc = acc + jnp.where(hit, 1.0, 0.0)
                    out.append(acc)
                return tuple(out)
            return body

        accs = tuple(jnp.zeros((COUNT_ROWS, LANES), F32) for _ in range(n_rc))
        accs = lax.fori_loop(0, n_full // COUNT_BLOCKS, body_of(COUNT_BLOCKS), accs)
        accs = lax.fori_loop(n_full // COUNT_BLOCKS * COUNT_BLOCKS, n_full, body_of(1), accs)
        if diag_tiles is not None:
            accs = list(accs)
            for r in range(n_rc):
                for j in range(diag_tiles[r]):
                    s = sc_ref[r * COUNT_ROWS:(r + 1) * COUNT_ROWS,
                               pl.ds(pl.multiple_of(n_full * tk + j * LANES, LANES), LANES)]
                    hit = (s > cands[r]) if strict else (s >= cands[r])
                    accs[r] = accs[r] + jnp.where(hit, 1.0, 0.0)
        return jnp.concatenate([jnp.sum(a, axis=1, keepdims=True) for a in accs], axis=0)

    kf = float(topk)
    cnt0 = count(jnp.zeros((tq, 1), F32), False)
    ok0 = cnt0 >= kf
    c0 = jnp.where(ok0, 0, INT_MIN).astype(I32)
    n0 = jnp.where(ok0, cnt0, 0.0)

    def search_body(i, st):
        c, n_ge = st
        cand = c | jnp.left_shift(jnp.int32(1), 30 - i)
        cnt = count(_key_to_float(cand), False)
        ok = cnt >= kf
        return jnp.where(ok, cand, c), jnp.where(ok, cnt, n_ge)

    ckey, n_ge = lax.fori_loop(0, 31, search_body, (c0, n0))
    thr = _key_to_float(jnp.maximum(ckey, KEY_NEG_FLT_MAX))
    need = kf - count(thr, True)
    ambiguous = jnp.max(jnp.where(n_ge > kf, 1.0, 0.0)) > 0.0

    @pl.when(jnp.logical_not(ambiguous))
    def _():
        def body(kb, carry):
            sc_ref[:, blk(kb)] = jnp.where(sc_ref[:, blk(kb)] >= thr, 0.0, NEG_BIG)
            return carry
        lax.fori_loop(0, nvis, body, 0)

    @pl.when(ambiguous)
    def _():
        tri = tri_ref[...]

        def body(kb, carries):
            out = []
            for r in range(tq // TIE_ROWS):
                rows = slice(r * TIE_ROWS, (r + 1) * TIE_ROWS)
                s = sc_ref[rows, blk(kb)]
                eq = s == thr[rows]
                eqf = jnp.where(eq, 1.0, 0.0)
                before = carries[r] + _dot(eqf.astype(BF16), tri)
                sel = (s > thr[rows]) | (eq & (before < need[rows]))
                sc_ref[rows, blk(kb)] = jnp.where(sel, 0.0, NEG_BIG)
                out.append(before[:, tk - 1:] + eqf[:, tk - 1:])
            return tuple(out)
        lax.fori_loop(0, nvis, body, tuple(jnp.zeros((TIE_ROWS, 1), F32) for _ in range(tq // TIE_ROWS)))

    q = q_ref[...]
    low_k = lax.broadcasted_iota(I32, (tk, LANES), 1) < HEAD_DIM
    for g in range(N_HEADS // ATTN_GROUP):
        heads = range(g * ATTN_GROUP, (g + 1) * ATTN_GROUP)
        qm = {}
        for j in range(g * ATTN_GROUP // 2, (g + 1) * ATTN_GROUP // 2):
            qm[2 * j], qm[2 * j + 1] = head_masked(q[:, j * LANES:(j + 1) * LANES])
        m_ref[...] = jnp.full((ATTN_GROUP, tq, LANES), NEG_BIG, F32)
        acc_ref[...] = jnp.zeros((ATTN_GROUP, tq, LANES), F32)

        def body(kb, carry, heads=heads, qm=qm):
            bias = sc_ref[:, blk(kb)]

            def logits(h):
                return _dot(qm[h], kt_ref[(h // 2) * LANES:(h // 2 + 1) * LANES, blk(kb)])

            lg_next = logits(heads[0])
            for n, h in enumerate(heads):
                lg = lg_next + bias
                if n + 1 < len(heads):
                    lg_next = logits(heads[n + 1])
                vv = v_ref[blk(kb), (h // 2) * LANES:(h // 2 + 1) * LANES]
                vh = jnp.where(low_k if h % 2 == 0 else jnp.logical_not(low_k), vv, one_b)
                m_old = m_ref[n]
                m_new = jnp.maximum(m_old, jnp.max(lg, axis=1, keepdims=True))
                p = jnp.concatenate(
                    [jnp.exp(lg[:, c * LANES:(c + 1) * LANES] - m_new) for c in range(n_sub)], axis=1)
                acc_ref[n] = jnp.exp(m_old - m_new) * acc_ref[n] + _dot(p.astype(BF16), vh)
                m_ref[n] = m_new
            return carry

        lax.fori_loop(0, nvis, body, 0)
        for n in range(0, ATTN_GROUP, 2):
            outs = []
            for e in range(2):
                acc = acc_ref[n + e]
                outs.append(acc / pltpu.roll(acc, HEAD_DIM, 1))
            j = (g * ATTN_GROUP + n) // 2
            o_ref[:, j * LANES:(j + 1) * LANES] = jnp.where(low, outs[0], outs[1]).astype(BF16)


def _dsa(q, iq, iw, kt, v, ikt, tri, *, tq, tk, topk):
    B, S, d_attn = q.shape
    once = dict(pipeline_mode=pl.Buffered(1))
    return pl.pallas_call(
        functools.partial(_dsa_kernel, tq=tq, tk=tk, topk=topk),
        grid=(B, S // tq),
        in_specs=[
            pl.BlockSpec((None, tq, d_attn), lambda b, i: (b, i, 0)),
            pl.BlockSpec((None, tq, 2 * LANES), lambda b, i: (b, i, 0)),
            pl.BlockSpec((None, tq, LANES), lambda b, i: (b, i, 0)),
            pl.BlockSpec((d_attn, S), lambda b, i: (0, b), **once),
            pl.BlockSpec((None, S, d_attn), lambda b, i: (b, 0, 0), **once),
            pl.BlockSpec((LANES, S), lambda b, i: (0, b), **once),
            pl.BlockSpec((tk, tk), lambda b, i: (0, 0), **once),
        ],
        out_specs=pl.BlockSpec((None, tq, d_attn), lambda b, i: (b, i, 0)),
        out_shape=jax.ShapeDtypeStruct((B, S, d_attn), BF16),
        scratch_shapes=[pltpu.VMEM((tq, S), F32), pltpu.VMEM((ATTN_GROUP, tq, LANES), F32), pltpu.VMEM((ATTN_GROUP, tq, LANES), F32)],
        compiler_params=_cparams("arbitrary", "arbitrary"),
        name="dsa",
    )(q, iq, iw, kt, v, ikt, tri)


def _mix_kernel(x_ref, u_ref, uh_ref, g1_ref, wg_ref, ya_ref, pw_ref, pb_ref, ps_ref, wa_ref, wp_ref, wo_ref,
                g2_ref, rw_ref, rb_ref, x1_ref, h2_ref, lg_ref, *, ts, tiles_per_seq, d_model):
    it = pl.program_id(0) % tiles_per_seq
    u = u_ref[...]
    halo = jnp.where(it == 0, 0.0, uh_ref[...])
    ext = jnp.concatenate([halo, u], axis=0)
    t_in_seq = it * ts + lax.broadcasted_iota(I32, (ts, 1), 0)
    ys = []
    for g, w in enumerate(POOL_WINDOWS):
        s = ext[:, g * LANES:(g + 1) * LANES]
        d = 1
        while d < w:
            s = s + pltpu.roll(s, d, 0)
            d *= 2
        cnt = jnp.minimum(t_in_seq + 1, w).astype(F32)
        y = s[POOL_HALO:, :] / cnt - u[:, g * LANES:(g + 1) * LANES]
        ys.append(_dot(y.astype(BF16), pw_ref[g]))
    y_pool = (jnp.concatenate(ys, axis=1) + pb_ref[...]) * ps_ref[...]
    x = x_ref[...]
    h = x * lax.rsqrt(jnp.mean(x * x, axis=-1, keepdims=True) + EPS) * g1_ref[...]
    sg = jax.nn.sigmoid(_dot(h.astype(BF16), wg_ref[...]))
    merged = (sg[:, :d_model] * _dot(ya_ref[...], wa_ref[...])
              + sg[:, d_model:] * _dot(y_pool.astype(BF16), wp_ref[...]))
    x1 = x + _dot(merged.astype(BF16), wo_ref[...])
    x1_ref[...] = x1
    h2 = x1 * lax.rsqrt(jnp.mean(x1 * x1, axis=-1, keepdims=True) + EPS) * g2_ref[...]
    h2b = h2.astype(BF16)
    h2_ref[...] = h2b
    lg_ref[...] = _dot(h2b, rw_ref[...]) + rb_ref[...]


def _mix(x2, u, g1, wg, ya, pw, pb, ps, wa, wp, wo, g2, rw, rb, *, ts, seq):
    T, d_model = x2.shape
    d_pool = u.shape[1]
    d_attn = ya.shape[1]
    tiles_per_seq = seq // ts
    halo_blocks = ts // POOL_HALO
    const2 = lambda i: (0, 0)
    row = lambda i: (i, 0)
    outs = [
        jax.ShapeDtypeStruct((T, d_model), F32),
        jax.ShapeDtypeStruct((T, d_model), BF16),
        jax.ShapeDtypeStruct((T, LANES), F32),
    ]
    return pl.pallas_call(
        functools.partial(_mix_kernel, ts=ts, tiles_per_seq=tiles_per_seq, d_model=d_model),
        grid=(T // ts,),
        in_specs=[
            pl.BlockSpec((ts, d_model), row),
            pl.BlockSpec((ts, d_pool), row),
            pl.BlockSpec((POOL_HALO, d_pool), lambda i: (jnp.maximum(i * halo_blocks - 1, 0), 0)),
            pl.BlockSpec((1, d_model), const2),
            pl.BlockSpec(wg.shape, const2),
            pl.BlockSpec((ts, d_attn), row),
            pl.BlockSpec(pw.shape, lambda i: (0, 0, 0)),
            pl.BlockSpec((1, d_pool), const2),
            pl.BlockSpec((1, d_pool), const2),
            pl.BlockSpec(wa.shape, const2),
            pl.BlockSpec(wp.shape, const2),
            pl.BlockSpec(wo.shape, const2),
            pl.BlockSpec((1, d_model), const2),
            pl.BlockSpec(rw.shape, const2),
            pl.BlockSpec((1, LANES), const2),
        ],
        out_specs=[pl.BlockSpec((ts, s.shape[1]), row) for s in outs],
        out_shape=outs,
        compiler_params=_cparams("arbitrary"),
        name="mix",
    )(x2, u, u, g1, wg, ya, pw, pb, ps, wa, wp, wo, g2, rw, rb)


def _router_kernel(lg_ref, lower_ref, upper_ref, lp_ref, w_ref, info_ref, carry_ref, *, ts):
    @pl.when(pl.program_id(0) == 0)
    def _():
        carry_ref[...] = jnp.zeros_like(carry_ref)

    l = lg_ref[...]
    lane = lax.broadcasted_iota(I32, (ts, LANES), 1)
    vals, hots = [], []
    for _ in range(TOP_K_EXPERTS):
        m = jnp.max(l, axis=1, keepdims=True)
        idx = jnp.min(jnp.where(l == m, lane, LANES), axis=1, keepdims=True)
        hot = lane == idx
        vals.append(m)
        hots.append(hot)
        l = jnp.where(hot, -jnp.inf, l)
    es = [jnp.exp(v - vals[0]) for v in vals]
    den = es[0] + es[1] + es[2] + es[3]
    picked = jnp.zeros((ts, LANES), F32)
    for hot in hots:
        picked = picked + jnp.where(hot, 1.0, 0.0)
    before = _dot(lower_ref[...], picked.astype(BF16))
    n = jnp.sum(picked, axis=0, keepdims=True)
    n_pad = jnp.floor((n + (ROW_ALIGN - 1)) * (1.0 / ROW_ALIGN)) * ROW_ALIGN
    run_start = _dot(jnp.broadcast_to(n_pad, (8, LANES)).astype(BF16), upper_ref[...])[:1]
    pos = before + run_start
    lp_o = jnp.zeros((ts, LANES), F32)
    w_o = jnp.zeros((ts, LANES), F32)
    for k in range(TOP_K_EXPERTS):
        slot = lane == k
        lp_o = jnp.where(slot, jnp.sum(jnp.where(hots[k], pos, 0.0), axis=1, keepdims=True), lp_o)
        w_o = jnp.where(slot, es[k] / den, w_o)
    lp_ref[...] = lp_o.astype(I32)
    w_ref[...] = w_o
    row = lax.broadcasted_iota(I32, (8, LANES), 0)
    info_ref[...] = jnp.where(row == 0, n_pad, jnp.where(row == 1, run_start, jnp.where(row == 2, carry_ref[...], 0.0)))
    carry_ref[...] = carry_ref[...] + n_pad


def _router(logits, lower, upper, *, ts):
    T = logits.shape[0]
    row = lambda i: (i, 0)
    const = lambda i: (0, 0)
    outs = [
        jax.ShapeDtypeStruct((T, LANES), I32),
        jax.ShapeDtypeStruct((T, LANES), F32),
        jax.ShapeDtypeStruct((T // ts * 8, LANES), F32),
    ]
    return pl.pallas_call(
        functools.partial(_router_kernel, ts=ts),
        grid=(T // ts,),
        in_specs=[pl.BlockSpec((ts, LANES), row), pl.BlockSpec((ts, ts), const), pl.BlockSpec((LANES, LANES), const)],
        out_specs=[pl.BlockSpec((ts, LANES), row), pl.BlockSpec((ts, LANES), row), pl.BlockSpec((8, LANES), row)],
        out_shape=outs,
        scratch_shapes=[pltpu.VMEM((1, LANES), F32)],
        compiler_params=_cparams("arbitrary"),
        name="router",
    )(logits, lower, upper)


def _run_chunks(largest):
    c, out = largest, []
    while c >= ROW_ALIGN:
        out.append(c)
        c //= 2
    return tuple(out)


def _for_each_piece(n, chunks, fn):
    def pieces(sizes, off):
        for c in sizes:
            @pl.when((n & c) != 0)
            def _(off=off, c=c):
                fn(off, c)
            off = off + (n & c)
        return off

    large = tuple(c for c in chunks if c >= LARGE_PIECE)
    small = tuple(c for c in chunks if c < LARGE_PIECE)
    if large:
        @pl.when(n >= LARGE_PIECE)
        def _():
            pieces(large, jnp.int32(0))
    pieces(small, n & ~jnp.int32(LARGE_PIECE - 1))


def _for_each_run_chunk(i, n_ref, ls_ref, go_ref, chunks, fn):
    for e in range(N_EXPERTS):
        ls = ls_ref[i * N_EXPERTS + e]
        go = go_ref[i * N_EXPERTS + e]
        _for_each_piece(
            n_ref[i * N_EXPERTS + e], chunks,
            lambda off, c, ls=ls, go=go: fn(pl.multiple_of(ls + off, ROW_ALIGN), pl.multiple_of(go + off, ROW_ALIGN), c))


def _dispatch_kernel(n_ref, ls_ref, go_ref, tail_ref, tail_len_ref, lpt_ref, h_ref, xs_ref, xl_ref, zero_ref, sems,
                     *, ts, lp_rows):
    i = pl.program_id(0)
    slot = i % 2
    chunks = _run_chunks(ts)
    tail_chunks = _run_chunks(zero_ref.shape[0])

    @pl.when(i == 0)
    def _():
        zero_ref[...] = jnp.zeros_like(zero_ref)
        for e in range(N_EXPERTS):
            for wait in (False, True):
                def piece(off, c, e=e, wait=wait):
                    cp = pltpu.make_async_copy(
                        zero_ref.at[pl.ds(0, c)],
                        xs_ref.at[pl.ds(pl.multiple_of(tail_ref[e] + off, ROW_ALIGN), c)], sems.at[2])
                    cp.wait() if wait else cp.start()
                _for_each_piece(tail_len_ref[e], tail_chunks, piece)

        def trailing(k):
            row = pl.multiple_of(tail_ref[N_EXPERTS] + k * zero_ref.shape[0], zero_ref.shape[0])
            return pltpu.make_async_copy(zero_ref, xs_ref.at[pl.ds(row, zero_ref.shape[0])], sems.at[2])

        def start(k, c):
            trailing(k).start()
            return c

        def wait(k, c):
            trailing(k).wait()
            return c

        lax.fori_loop(0, tail_len_ref[N_EXPERTS], start, 0)
        lax.fori_loop(0, tail_len_ref[N_EXPERTS], wait, 0)

    h = h_ref[...]
    xl = xl_ref.at[slot]
    for r in range(lp_rows // PERM_ROWS):
        prow = r * PERM_ROWS + lax.broadcasted_iota(I32, (PERM_ROWS, ts), 0)
        hit = prow == lpt_ref[0:1, :]
        for k in range(1, TOP_K_EXPERTS):
            hit = hit | (prow == lpt_ref[k:k + 1, :])
        xl[r * PERM_ROWS:(r + 1) * PERM_ROWS, :] = _dot(jnp.where(hit, 1.0, 0.0).astype(BF16), h)

    def copy(buf, local_row, global_row, rows):
        return pltpu.make_async_copy(
            xl_ref.at[buf, pl.ds(local_row, rows)], xs_ref.at[pl.ds(global_row, rows)], sems.at[buf])

    _for_each_run_chunk(i, n_ref, ls_ref, go_ref, chunks, lambda a, b, c: copy(slot, a, b, c).start())

    @pl.when(i > 0)
    def _():
        _for_each_run_chunk(i - 1, n_ref, ls_ref, go_ref, chunks, lambda a, b, c: copy(1 - slot, a, b, c).wait())

    @pl.when(i == pl.num_programs(0) - 1)
    def _():
        _for_each_run_chunk(i, n_ref, ls_ref, go_ref, chunks, lambda a, b, c: copy(slot, a, b, c).wait())


def _dispatch(n_tab, ls_tab, go_tab, tail_tab, tail_len_tab, lpt, h2, *, ts, lp_rows, rows_out, tm):
    T, d = h2.shape
    grid_spec = pltpu.PrefetchScalarGridSpec(
        num_scalar_prefetch=5,
        grid=(T // ts,),
        in_specs=[
            pl.BlockSpec((8, ts), lambda i, *_: (0, i)),
            pl.BlockSpec((ts, d), lambda i, *_: (i, 0)),
        ],
        out_specs=pl.BlockSpec(memory_space=pl.ANY),
        scratch_shapes=[pltpu.VMEM((2, lp_rows, d), F32), pltpu.VMEM((tm // 2, d), F32),
                        pltpu.SemaphoreType.DMA((3,))],
    )
    return pl.pallas_call(
        functools.partial(_dispatch_kernel, ts=ts, lp_rows=lp_rows),
        grid_spec=grid_spec,
        out_shape=jax.ShapeDtypeStruct((rows_out, d), F32),
        compiler_params=_cparams("arbitrary"),
        name="dispatch",
    )(n_tab, ls_tab, go_tab, tail_tab, tail_len_tab, lpt, h2)


def _expert_kernel(te_ref, tv_ref, nv_ref, x_ref, w1_ref, b1_ref, w2_ref, b2_ref, y_ref, w1b_ref, w2b_ref, *, d_expert):
    del nv_ref
    i = pl.program_id(0)

    @pl.when(jnp.logical_or(i == 0, te_ref[i] != te_ref[jnp.maximum(i - 1, 0)]))
    def _():
        w1b_ref[...] = w1_ref[...].astype(BF16)
        w2b_ref[...] = w2_ref[...].astype(BF16)

    @pl.when(tv_ref[i] == 1)
    def _():
        gu = _dot(x_ref[...].astype(BF16), w1b_ref[...]) + b1_ref[...]
        gate = jnp.minimum(gu[:, :d_expert], SWIGLU_LIMIT)
        lin = jnp.clip(gu[:, d_expert:], -SWIGLU_LIMIT, SWIGLU_LIMIT)
        act = (lin + 1.0) * gate * jax.nn.sigmoid(SWIGLU_ALPHA * gate)
        y_ref[...] = _dot(act.astype(BF16), w2b_ref[...]) + b2_ref[...]

    @pl.when(tv_ref[i] == 0)
    def _():
        y_ref[...] = jnp.zeros_like(y_ref)


def _experts(tile_expert, tile_valid, n_valid, xs, w1, b1, w2, b2, *, tm):
    P, d = xs.shape
    d_expert = w2.shape[1]
    grid_spec = pltpu.PrefetchScalarGridSpec(
        num_scalar_prefetch=3,
        grid=(P // tm,),
        in_specs=[
            pl.BlockSpec((tm, d), lambda i, te, tv, nv: (jnp.minimum(i, nv[0] - 1), 0)),
            pl.BlockSpec((None, d, 2 * d_expert), lambda i, te, tv, nv: (te[i], 0, 0)),
            pl.BlockSpec((None, 1, 2 * d_expert), lambda i, te, tv, nv: (te[i], 0, 0)),
            pl.BlockSpec((None, d_expert, d), lambda i, te, tv, nv: (te[i], 0, 0)),
            pl.BlockSpec((None, 1, d), lambda i, te, tv, nv: (te[i], 0, 0)),
        ],
        out_specs=pl.BlockSpec((tm, d), lambda i, te, tv, nv: (i, 0)),
        scratch_shapes=[pltpu.VMEM((d, 2 * d_expert), BF16), pltpu.VMEM((d_expert, d), BF16)],
    )
    return pl.pallas_call(
        functools.partial(_expert_kernel, d_expert=d_expert),
        grid_spec=grid_spec,
        out_shape=jax.ShapeDtypeStruct((P, d), F32),
        compiler_params=_cparams("arbitrary"),
        name="experts",
    )(tile_expert, tile_valid, n_valid, xs, w1, b1, w2, b2)


def _combine_kernel(n_ref, ls_ref, go_ref, lp_ref, w_ref, x1_ref, y_ref, o_ref, yl_ref, sems, *, ts, lp_rows):
    i = pl.program_id(0)
    slot = i % 2
    chunks = _run_chunks(ts)

    def copy(buf, local_row, global_row, rows):
        return pltpu.make_async_copy(
            y_ref.at[pl.ds(global_row, rows)], yl_ref.at[buf, pl.ds(local_row, rows)], sems.at[buf])

    @pl.when(i == 0)
    def _():
        yl_ref[...] = jnp.zeros_like(yl_ref)
        _for_each_run_chunk(i, n_ref, ls_ref, go_ref, chunks, lambda a, b, c: copy(slot, a, b, c).start())

    @pl.when(i < pl.num_programs(0) - 1)
    def _():
        _for_each_run_chunk(i + 1, n_ref, ls_ref, go_ref, chunks, lambda a, b, c: copy(1 - slot, a, b, c).start())

    _for_each_run_chunk(i, n_ref, ls_ref, go_ref, chunks, lambda a, b, c: copy(slot, a, b, c).wait())

    lp = lp_ref[...]
    w = w_ref[...]
    lp_b = [jnp.broadcast_to(lp[:, k:k + 1], (ts, PERM_ROWS)) for k in range(TOP_K_EXPERTS)]
    w_b = [jnp.broadcast_to(w[:, k:k + 1], (ts, PERM_ROWS)) for k in range(TOP_K_EXPERTS)]
    yl = yl_ref.at[slot]
    out = x1_ref[...]
    for r in range(lp_rows // PERM_ROWS):
        pcol = r * PERM_ROWS + lax.broadcasted_iota(I32, (ts, PERM_ROWS), 1)
        wm = jnp.zeros((ts, PERM_ROWS), F32)
        for k in range(TOP_K_EXPERTS):
            wm = wm + jnp.where(pcol == lp_b[k], w_b[k], 0.0)
        out = out + _dot(wm.astype(BF16), yl[r * PERM_ROWS:(r + 1) * PERM_ROWS, :].astype(BF16))
    o_ref[...] = out


def _combine(n_tab, ls_tab, go_tab, lp, wts, x1, y, *, ts, lp_rows):
    T, d = x1.shape
    grid_spec = pltpu.PrefetchScalarGridSpec(
        num_scalar_prefetch=3,
        grid=(T // ts,),
        in_specs=[
            pl.BlockSpec((ts, LANES), lambda i, *_: (i, 0)),
            pl.BlockSpec((ts, LANES), lambda i, *_: (i, 0)),
            pl.BlockSpec((ts, d), lambda i, *_: (i, 0)),
            pl.BlockSpec(memory_space=pl.ANY),
        ],
        out_specs=pl.BlockSpec((ts, d), lambda i, *_: (i, 0)),
        scratch_shapes=[pltpu.VMEM((2, lp_rows, d), F32), pltpu.SemaphoreType.DMA((2,))],
    )
    return pl.pallas_call(
        functools.partial(_combine_kernel, ts=ts, lp_rows=lp_rows),
        grid_spec=grid_spec,
        out_shape=jax.ShapeDtypeStruct((T, d), F32),
        compiler_params=_cparams("arbitrary"),
        name="combine",
    )(n_tab, ls_tab, go_tab, lp, wts, x1, y)


def _pick(n, prefs):
    for p in prefs:
        if n % p == 0:
            return p
    raise ValueError(f"no tile in {prefs} divides {n}")


def _layer(x, attn_norm_g, w_in, q_norm_g, k_norm_g, pool_w, pool_b, pool_scale, w_branch_attn,
           w_branch_pool, w_out, ffn_norm_g, router_w, router_b, w1, b1, w2, b2):
    B, S, d_model = x.shape
    T = B * S
    d_attn = N_HEADS * HEAD_DIM
    d_pool = len(POOL_WINDOWS) * LANES
    d_idx = N_IDX_HEADS * IDX_DIM
    assert d_attn == d_model // 2 and d_pool == d_model // 2 and d_idx == 2 * LANES
    assert w_in.shape[1] == 3 * d_attn + d_idx + IDX_DIM + N_IDX_HEADS + d_pool + 2 * d_model

    o = 3 * d_attn
    w_iq = w_in[:, o:o + d_idx]
    w_ik = w_in[:, o + d_idx:o + d_idx + IDX_DIM]
    w_iw = w_in[:, o + d_idx + IDX_DIM:o + d_idx + IDX_DIM + N_IDX_HEADS]
    o2 = o + d_idx + IDX_DIM + N_IDX_HEADS
    w_cat = jnp.concatenate([
        w_in[:, :o], w_iq,
        jnp.pad(w_ik, ((0, 0), (0, LANES - IDX_DIM))),
        jnp.pad(w_iw, ((0, 0), (0, LANES - N_IDX_HEADS))),
        w_in[:, o2:o2 + d_pool]], axis=1).astype(BF16)
    w_gates = w_in[:, o2 + d_pool:].astype(BF16)
    head_of = jnp.arange(d_attn) // HEAD_DIM
    bd = (head_of[:, None] == head_of[None, :]).astype(BF16)

    x2 = x.reshape(T, d_model)
    q, kt, v, iq, ikt2, iw, u = _inproj(
        x2, attn_norm_g.reshape(1, d_model), w_cat,
        jnp.tile(q_norm_g, N_HEADS).reshape(1, d_attn), jnp.tile(k_norm_g, N_HEADS).reshape(1, d_attn), bd,
        tm=_pick(T, (512, 256, 128)), d_attn=d_attn, d_pool=d_pool, d_model=d_model)

    topk = min(TOPK_MAX, S // 4)
    tq = _pick(S, (512, 256, 128))
    tk = _pick(S, (512, 256, 128))
    tri = (jnp.arange(tk)[:, None] < jnp.arange(tk)[None, :]).astype(BF16)
    y_attn = _dsa(q.reshape(B, S, d_attn), iq.reshape(B, S, 2 * LANES), iw.reshape(B, S, LANES),
                  kt, v.reshape(B, S, d_attn), ikt2, tri, tq=tq, tk=tk, topk=topk)

    rw = jnp.pad(router_w, ((0, 0), (0, LANES - N_EXPERTS))).astype(BF16)
    rb = jnp.concatenate([router_b.astype(F32), jnp.full((LANES - N_EXPERTS,), NEG_BIG, F32)]).reshape(1, LANES)
    x1, h2, logits = _mix(
        x2, u, attn_norm_g.reshape(1, d_model), w_gates, y_attn.reshape(T, d_attn), pool_w.astype(BF16),
        pool_b.reshape(1, d_pool),
        pool_scale.reshape(1, d_pool), w_branch_attn.astype(BF16), w_branch_pool.astype(BF16),
        w_out.astype(BF16), ffn_norm_g.reshape(1, d_model), rw, rb, ts=_pick(S, (256, 128)), seq=S)

    tr = _pick(T, (512, 256, 128))
    n_tok_tiles = T // tr
    lp_rows = tr * TOP_K_EXPERTS + N_EXPERTS * ROW_ALIGN
    assert lp_rows % PERM_ROWS == 0
    lower = (jnp.arange(tr)[:, None] > jnp.arange(tr)[None, :]).astype(BF16)
    upper = (jnp.arange(LANES)[:, None] < jnp.arange(LANES)[None, :]).astype(BF16)
    lp, wts, info = _router(logits, lower, upper, ts=tr)
    info = info.reshape(n_tok_tiles, 8, LANES)[:, :, :N_EXPERTS].astype(I32)
    run_len, run_start, rows_before = info[:, 0], info[:, 1], info[:, 2]
    tm = EXPERT_ROWS
    total = rows_before[-1] + run_len[-1]
    group = (total + tm - 1) // tm * tm
    ends = jnp.cumsum(group)
    run_global = (ends - group)[None, :] + rows_before
    max_rows = T * TOP_K_EXPERTS + n_tok_tiles * N_EXPERTS * (ROW_ALIGN - 1) + N_EXPERTS * (tm - 1)
    n_tiles = (max_rows + tm - 1) // tm
    tile_row = jnp.arange(n_tiles, dtype=I32) * tm
    tile_valid = (tile_row < ends[-1]).astype(I32)
    tile_expert = jnp.minimum(jnp.sum((tile_row[:, None] >= ends[None, :]).astype(I32), axis=1), N_EXPERTS - 1)
    n_valid = (ends[-1:] // tm).astype(I32)
    tabs = (run_len.reshape(-1), run_start.reshape(-1), run_global.reshape(-1))

    lpt = jnp.pad(lp[:, :TOP_K_EXPERTS].T, ((0, 8 - TOP_K_EXPERTS), (0, 0)), constant_values=-1)
    rows_out = n_tiles * tm
    fill_start = jnp.concatenate([ends - group + total, ends[-1:]])
    fill_len = jnp.concatenate([group - total, (rows_out - ends[-1:]) // (tm // 2)])
    xs = _dispatch(*tabs, fill_start, fill_len, lpt, h2, ts=tr, lp_rows=lp_rows, rows_out=rows_out, tm=tm)
    y = _experts(tile_expert, tile_valid, n_valid, xs, w1, b1.reshape(N_EXPERTS, 1, -1),
                 w2, b2.reshape(N_EXPERTS, 1, -1), tm=tm)
    out = _combine(*tabs, lp, wts, x1, y, ts=tr, lp_rows=lp_rows)
    return out.reshape(B, S, d_model)


def kernel(x, attn_norm_g, w_in, q_norm_g, k_norm_g, pool_w, pool_b, pool_scale, w_branch_attn, w_branch_pool,
           w_out, ffn_norm_g, router_w, router_b, expert_w1, expert_b1, expert_w2, expert_b2):
    for l in range(attn_norm_g.shape[0]):
        x = _layer(x, attn_norm_g[l], w_in[l], q_norm_g[l], k_norm_g[l], pool_w[l], pool_b[l], pool_scale[l],
                   w_branch_attn[l], w_branch_pool[l], w_out[l], ffn_norm_g[l], router_w[l], router_b[l],
                   expert_w1[l], expert_b1[l], expert_w2[l], expert_b2[l])
    return x
```
